```python
import jax, jax.numpy as jnp
from jax import lax
import numpy as np

D_MODEL = 1024
BATCH = 32
SEQ = 2048
DEPTH = 1
DEC_BATCH = 4
DEC_SEQ = 4096
PAST_LEN = 128

N_META = 16
GRID_W = 64
WIN_ROWS_MAX = 8
WIN_COLS = 16
Q_BLOCK_COLS = 16
K_BLOCK_COLS = Q_BLOCK_COLS + WIN_COLS
CONV_WIDTH = 31
D_CONV = D_MODEL
N_HEADS = 16
HEAD_DIM = 64
D_ATT = N_HEADS * HEAD_DIM
D_MIX = D_CONV + D_ATT
D_IN = 3 * D_CONV + 4 * D_ATT
RMS_EPS = 1e-6
LN_EPS = 1e-5
NEG_INF = -1e30

kernel_name = "hybrid_conformer_natten_encoder"


def rms_norm(x, g):
    x32 = x.astype(jnp.float32)
    y = x32 * lax.rsqrt(jnp.mean(x32 * x32, axis=-1, keepdims=True) + RMS_EPS)
    return (y * g.astype(jnp.float32)).astype(x.dtype)


def layer_norm(x, g, b):
    x32 = x.astype(jnp.float32)
    mu = jnp.mean(x32, axis=-1, keepdims=True)
    xc = x32 - mu
    var = jnp.mean(xc * xc, axis=-1, keepdims=True)
    y = xc * lax.rsqrt(var + LN_EPS) * g.astype(jnp.float32) + b.astype(jnp.float32)
    return y.astype(x.dtype)


def conv_module(val, glu_gate, conv_w, conv_b, ln_g, ln_b):
    u = val * jax.nn.sigmoid(glu_gate)
    pad = CONV_WIDTH // 2
    y = lax.conv_general_dilated(
        u, conv_w.astype(u.dtype)[:, None, :], window_strides=(1,),
        padding=[(pad, pad)], dimension_numbers=('NWC', 'WIO', 'NWC'),
        feature_group_count=u.shape[-1])
    y = y + conv_b.astype(u.dtype)
    y = layer_norm(y, ln_g, ln_b)
    return jax.nn.silu(y)


def _column_blocks():
    n_cb = GRID_W // Q_BLOCK_COLS
    j = np.arange(n_cb)
    kc0 = np.clip(j * Q_BLOCK_COLS - WIN_COLS // 2, 0, GRID_W - K_BLOCK_COLS)
    key_cols = kc0[:, None] + np.arange(K_BLOCK_COLS)[None, :]
    q_cols = j[:, None] * Q_BLOCK_COLS + np.arange(Q_BLOCK_COLS)[None, :]
    start = np.clip(q_cols - WIN_COLS // 2, 0, GRID_W - WIN_COLS)
    off = key_cols[:, None, :] - start[:, :, None]
    valid = (off >= 0) & (off < WIN_COLS)
    rel = key_cols[:, None, :] - q_cols[:, :, None]
    rel_idx = np.clip(rel + WIN_COLS - 1, 0, 2 * WIN_COLS - 2)
    return key_cols, valid, rel_idx


def neighbourhood_attention(q, k, v, q_meta, k_meta, v_meta, rel_bias):
    B, T, H, dh = q.shape
    rows = T // GRID_W
    wr = min(WIN_ROWS_MAX, rows)
    n_cb = GRID_W // Q_BLOCK_COLS
    scale = dh ** -0.5
    qg = (q * scale).reshape(B, rows, n_cb, Q_BLOCK_COLS, H, dh)
    kg = k.reshape(B, rows, GRID_W, H, dh)
    vg = v.reshape(B, rows, GRID_W, H, dh)
    key_cols, valid, rel_idx = _column_blocks()
    bias_c = rel_bias.astype(jnp.float32)[:, :, rel_idx]
    mask_add = jnp.asarray(np.where(valid, 0.0, NEG_INF), jnp.float32)
    k_meta_s = k_meta

    def one_row(r):
        sr = jnp.clip(r - wr // 2, 0, rows - wr)
        k_rows = lax.dynamic_slice_in_dim(kg, sr, wr, axis=1)
        v_rows = lax.dynamic_slice_in_dim(vg, sr, wr, axis=1)
        k_blk = k_rows[:, :, key_cols]
        v_blk = v_rows[:, :, key_cols]
        q_r = lax.dynamic_index_in_dim(qg, r, axis=1, keepdims=False)
        s = jnp.einsum('bjqhd,brjkhd->bhjqrk', q_r, k_blk,
                       preferred_element_type=jnp.float32)
        row_idx = sr + jnp.arange(wr) - r + (WIN_ROWS_MAX - 1)
        bias = jnp.take(bias_c, row_idx, axis=1)
        s = s + bias.transpose(0, 2, 3, 1, 4)[None] + mask_add[:, :, None, :]
        s = s.reshape(B, H, n_cb, Q_BLOCK_COLS, wr * K_BLOCK_COLS)
        s_meta = jnp.einsum('bjqhd,bmhd->bhjqm', q_r, k_meta_s,
                            preferred_element_type=jnp.float32)
        p = jax.nn.softmax(jnp.concatenate([s, s_meta], axis=-1), axis=-1).astype(v.dtype)
        p_grid = p[..., :wr * K_BLOCK_COLS].reshape(B, H, n_cb, Q_BLOCK_COLS, wr, K_BLOCK_COLS)
        p_meta = p[..., wr * K_BLOCK_COLS:]
        o = (jnp.einsum('bhjqrk,brjkhd->bjqhd', p_grid, v_blk)
             + jnp.einsum('bhjqm,bmhd->bjqhd', p_meta, v_meta))
        return o.reshape(B, GRID_W, H, dh)

    o_grid = lax.map(one_row, jnp.arange(rows))
    o_grid = o_grid.transpose(1, 0, 2, 3, 4).reshape(B, T, H, dh)
    s_mm = jnp.einsum('bqhd,bkhd->bhqk', q_meta * scale, k_meta,
                      preferred_element_type=jnp.float32)
    p_mm = jax.nn.softmax(s_mm, axis=-1).astype(v.dtype)
    o_meta = jnp.einsum('bhqk,bkhd->bqhd', p_mm, v_meta)
    return o_meta, o_grid


def encoder_layer(h, pre_g, w_in, conv_w, conv_b, ln_g, ln_b, rel_bias, post_g, w_out):
    B, L, _ = h.shape
    u = rms_norm(h, pre_g)
    z = jnp.einsum('bld,de->ble', u, w_in.astype(u.dtype))
    c_val, c_glu, c_gate, q, k, v, a_gate = jnp.split(
        z, [D_CONV, 2 * D_CONV, 3 * D_CONV, 3 * D_CONV + D_ATT,
            3 * D_CONV + 2 * D_ATT, 3 * D_CONV + 3 * D_ATT], axis=-1)
    conv_out = conv_module(c_val, c_glu, conv_w, conv_b, ln_g, ln_b) * jax.nn.silu(c_gate)
    q = q.reshape(B, L, N_HEADS, HEAD_DIM)
    k = k.reshape(B, L, N_HEADS, HEAD_DIM)
    v = v.reshape(B, L, N_HEADS, HEAD_DIM)
    o_meta, o_grid = neighbourhood_attention(
        q[:, N_META:], k[:, N_META:], v[:, N_META:],
        q[:, :N_META], k[:, :N_META], v[:, :N_META], rel_bias)
    att = jnp.concatenate([o_meta, o_grid], axis=1).reshape(B, L, D_ATT) * jax.nn.silu(a_gate)
    mixed = jnp.concatenate([conv_out, att], axis=-1)
    y = jnp.einsum('ble,ed->bld', mixed, w_out.astype(mixed.dtype))
    return h + rms_norm(y, post_g)


def run_trunk(x, meta_tokens, pre_norm_g, w_in, conv_w, conv_b, conv_ln_g, conv_ln_b,
              rel_bias, post_norm_g, w_out):
    B = x.shape[0]
    meta = jnp.broadcast_to(meta_tokens.astype(x.dtype)[None], (B, N_META, x.shape[-1]))
    h = jnp.concatenate([meta, x], axis=1)
    for i in range(DEPTH):
        h = encoder_layer(h, pre_norm_g[i], w_in[i], conv_w[i], conv_b[i], conv_ln_g[i],
                          conv_ln_b[i], rel_bias[i], post_norm_g[i], w_out[i])
    return h[:, N_META:]


def setup_inputs(seed: int = 0) -> dict:
    key = jax.random.key(seed)
    ks = jax.random.split(key, 13)
    f32 = jnp.float32
    nrm = lambda k, s: jax.random.normal(k, s, dtype=f32)
    return {
        "x_prompt": nrm(ks[0], (BATCH, SEQ, D_MODEL)),
        "x_sample": nrm(ks[1], (DEC_BATCH, DEC_SEQ, D_MODEL)),
        "meta_tokens": nrm(ks[2], (N_META, D_MODEL)),
        "pre_norm_g": 1.0 + 0.05 * nrm(ks[3], (DEPTH, D_MODEL)),
        "w_in": nrm(ks[4], (DEPTH, D_MODEL, D_IN)) * D_MODEL ** -0.5,
        "conv_w": nrm(ks[5], (DEPTH, CONV_WIDTH, D_CONV)) * CONV_WIDTH ** -0.5,
        "conv_b": 0.02 * nrm(ks[6], (DEPTH, D_CONV)),
        "conv_ln_g": 1.0 + 0.05 * nrm(ks[7], (DEPTH, D_CONV)),
        "conv_ln_b": 0.02 * nrm(ks[8], (DEPTH, D_CONV)),
        "rel_bias": 0.1 * nrm(ks[9], (DEPTH, N_HEADS, 2 * WIN_ROWS_MAX - 1, 2 * WIN_COLS - 1)),
        "post_norm_g": 1.0 + 0.05 * nrm(ks[10], (DEPTH, D_MODEL)),
        "w_out": nrm(ks[11], (DEPTH, D_MIX, D_MODEL)) * D_MIX ** -0.5,
    }


def reference(x_prompt, x_sample, meta_tokens, pre_norm_g, w_in, conv_w, conv_b,
              conv_ln_g, conv_ln_b, rel_bias, post_norm_g, w_out):
    y_prompt = run_trunk(x_prompt, meta_tokens, pre_norm_g, w_in, conv_w, conv_b,
                         conv_ln_g, conv_ln_b, rel_bias, post_norm_g, w_out)
    y_sample = run_trunk(x_sample, meta_tokens, pre_norm_g, w_in, conv_w, conv_b,
                         conv_ln_g, conv_ln_b, rel_bias, post_norm_g, w_out)
    return (y_prompt, y_sample)
```

```python
import functools

import numpy as np
import jax
import jax.numpy as jnp
from jax import lax
from jax.experimental import pallas as pl
from jax.experimental.pallas import tpu as pltpu

F32 = jnp.float32
BF16 = jnp.bfloat16

D_MODEL = 1024
N_META = 16
GRID_W = 64
WIN_ROWS = 8
WIN_COLS = 16
CONV_WIDTH = 31
CONV_PAD = CONV_WIDTH // 2
N_HEADS = 16
HEAD_DIM = 64
N_PARTS = 7
RMS_EPS = 1e-6
LN_EPS = 1e-5
NEG_INF = -1e30

HEADS_PER_GROUP = 4
GROUP_LANES = HEADS_PER_GROUP * HEAD_DIM
N_GROUPS = N_HEADS // HEADS_PER_GROUP
ROW_PAIRS = WIN_ROWS // 2
N_BIAS_ROWS = 2 * WIN_ROWS - 2

PROJ_TOKENS = 512
TILE_ROWS = 8
TILE_TOKENS = TILE_ROWS * GRID_W
KV_HALO = 4 * GRID_W
KV_WIN = TILE_TOKENS + 2 * KV_HALO
CONV_HALO = 16
CONV_CHUNK = 64
CONV_LANES = 256

VMEM_LIMIT = 58 * 1024 * 1024


def _sigmoid(x):
    return 1.0 / (1.0 + jnp.exp(-x))


def _proj_kernel(x_ref, g_ref, w_ref, u_ref, cg_ref, q_ref, k_ref, v_ref, ag_ref):
    x = x_ref[...]
    ms = jnp.mean(x * x, axis=-1, keepdims=True)
    xn = (x * lax.rsqrt(ms + RMS_EPS) * g_ref[...]).astype(BF16)

    def part(p):
        return jnp.dot(xn, w_ref[:, p * D_MODEL:(p + 1) * D_MODEL],
                       preferred_element_type=F32)

    u_ref[...] = (part(0) * _sigmoid(part(1))).astype(BF16)
    z = part(2)
    cg_ref[...] = (z * _sigmoid(z)).astype(BF16)
    q_ref[...] = (part(3) * (HEAD_DIM ** -0.5)).astype(BF16)
    k_ref[...] = part(4).astype(BF16)
    v_ref[...] = part(5).astype(BF16)
    z = part(6)
    ag_ref[...] = (z * _sigmoid(z)).astype(BF16)


def _project(x2d, pre_g, w_in_bf, tm):
    n = x2d.shape[0]
    assert n % tm == 0
    out = jax.ShapeDtypeStruct((n, D_MODEL), BF16)
    tok = pl.BlockSpec((tm, D_MODEL), lambda i: (i, 0))
    const = lambda shape: pl.BlockSpec(shape, lambda i: (0, 0),
                                       pipeline_mode=pl.Buffered(1))
    return pl.pallas_call(
        _proj_kernel,
        grid=(n // tm,),
        in_specs=[tok, const((1, D_MODEL)), const((D_MODEL, N_PARTS * D_MODEL))],
        out_specs=[tok] * 6,
        out_shape=[out] * 6,
        compiler_params=pltpu.CompilerParams(
            dimension_semantics=("arbitrary",), vmem_limit_bytes=VMEM_LIMIT),
        name="in_proj",
    )(x2d, pre_g, w_in_bf)


def _mixer_kernel(rows,
                  x_ref, u_ref, up_ref, un_ref, cg_ref, q_ref, ag_ref,
                  k_ref, kp_ref, kn_ref, v_ref, vp_ref, vn_ref,
                  um_ref, km_ref, vm_ref, cw_ref, cb_ref, lg_ref, lb_ref,
                  tbl_ref, hm_ref, pg_ref, wo_ref,
                  o_ref,
                  ubuf, ybuf, kwin, vwin, mixed):
    i = pl.program_id(1)
    last = pl.num_programs(1) - 1

    ubuf[CONV_HALO:CONV_HALO + TILE_TOKENS, :] = u_ref[...].astype(F32)

    @pl.when(i == 0)
    def _():
        ubuf[0:CONV_HALO, :] = um_ref[...].astype(F32)

    @pl.when(i > 0)
    def _():
        ubuf[0:CONV_HALO, :] = up_ref[...].astype(F32)

    @pl.when(i == last)
    def _():
        ubuf[CONV_HALO + TILE_TOKENS:, :] = jnp.zeros((CONV_HALO, D_MODEL), F32)

    @pl.when(i < last)
    def _():
        ubuf[CONV_HALO + TILE_TOKENS:, :] = un_ref[...].astype(F32)

    def conv_chunk(c):
        base = c * CONV_CHUNK
        for lc in range(D_MODEL // CONV_LANES):
            ls = slice(lc * CONV_LANES, (lc + 1) * CONV_LANES)
            acc = jnp.zeros((CONV_CHUNK, CONV_LANES), F32)
            for t in range(CONV_WIDTH):
                tap = ubuf[pl.ds(base + (CONV_HALO - CONV_PAD + t), CONV_CHUNK), ls]
                acc = acc + tap * cw_ref[t:t + 1, ls]
            ybuf[:, ls] = acc + cb_ref[:, ls]
        y = ybuf[...]
        mu = jnp.mean(y, axis=-1, keepdims=True)
        yc = y - mu
        var = jnp.mean(yc * yc, axis=-1, keepdims=True)
        yn = yc * lax.rsqrt(var + LN_EPS) * lg_ref[...] + lb_ref[...]
        act = yn * _sigmoid(yn)
        gate = cg_ref[pl.ds(base, CONV_CHUNK), :].astype(F32)
        mixed[pl.ds(base, CONV_CHUNK), 0:D_MODEL] = (act * gate).astype(BF16)

    for c in range(TILE_TOKENS // CONV_CHUNK):
        conv_chunk(c)

    kwin[0:KV_HALO, :] = kp_ref[...]
    kwin[KV_HALO:KV_HALO + TILE_TOKENS, :] = k_ref[...]
    kwin[KV_HALO + TILE_TOKENS:, :] = kn_ref[...]
    vwin[0:KV_HALO, :] = vp_ref[...]
    vwin[KV_HALO:KV_HALO + TILE_TOKENS, :] = v_ref[...]
    vwin[KV_HALO + TILE_TOKENS:, :] = vn_ref[...]

    r0 = i * TILE_ROWS
    lane_head = lax.broadcasted_iota(jnp.int32, (GRID_W, GROUP_LANES), 1) // HEAD_DIM
    contract_last = (((1,), (1,)), ((), ()))

    def row_body(j, carry):
        r = r0 + j
        sr = jnp.clip(r - WIN_ROWS // 2, 0, rows - WIN_ROWS)
        off = pl.multiple_of((sr - r0 + WIN_ROWS // 2) * GRID_W, GRID_W)
        d7 = sr - r + (WIN_ROWS - 1)
        qoff = pl.multiple_of(j * GRID_W, GRID_W)
        for g in range(N_GROUPS):
            ls = slice(g * GROUP_LANES, (g + 1) * GROUP_LANES)
            qg = q_ref[pl.ds(qoff, GRID_W), ls]
            qs = jnp.concatenate([qg] * HEADS_PER_GROUP, axis=0) * hm_ref[...]
            kt = kwin[pl.ds(off, WIN_ROWS * GRID_W), ls]
            s = lax.dot_general(qs, kt, contract_last, preferred_element_type=F32)
            bias = jnp.concatenate(
                [jnp.concatenate(
                    [tbl_ref[g * HEADS_PER_GROUP + hl, d7 + 2 * p] for p in range(ROW_PAIRS)],
                    axis=1) for hl in range(HEADS_PER_GROUP)], axis=0)
            s = s + bias
            sm = lax.dot_general(qs, km_ref[:, ls], contract_last,
                                 preferred_element_type=F32)
            m = jnp.maximum(jnp.max(s, axis=1, keepdims=True),
                            jnp.max(sm, axis=1, keepdims=True))
            p = jnp.exp(s - m)
            pm = jnp.exp(sm - m)
            l = jnp.sum(p, axis=1, keepdims=True) + jnp.sum(pm, axis=1, keepdims=True)
            vt = vwin[pl.ds(off, WIN_ROWS * GRID_W), ls]
            o = (jnp.dot(p.astype(BF16), vt, preferred_element_type=F32)
                 + jnp.dot(pm.astype(BF16), vm_ref[:, ls], preferred_element_type=F32))
            o = o * (1.0 / l)
            og = o[(HEADS_PER_GROUP - 1) * GRID_W:]
            for hl in range(HEADS_PER_GROUP - 2, -1, -1):
                og = jnp.where(lane_head == hl, o[hl * GRID_W:(hl + 1) * GRID_W], og)
            gate = ag_ref[pl.ds(qoff, GRID_W), ls].astype(F32)
            mixed[pl.ds(qoff, GRID_W), D_MODEL + g * GROUP_LANES:
                  D_MODEL + (g + 1) * GROUP_LANES] = (og * gate).astype(BF16)
        return carry

    lax.fori_loop(0, TILE_ROWS, row_body, 0)

    y = jnp.dot(mixed[...], wo_ref[...], preferred_element_type=F32)
    ms = jnp.mean(y * y, axis=-1, keepdims=True)
    o_ref[...] = x_ref[...] + y * lax.rsqrt(ms + RMS_EPS) * pg_ref[...]


def _mixer(x, parts, meta, conv_w, conv_b, ln_g, ln_b, table, head_mask, post_g, w_out_bf):
    b, t, _ = x.shape
    rows = t // GRID_W
    assert t % TILE_TOKENS == 0 and rows >= WIN_ROWS
    u, cg, q, k, v, ag = parts
    um, km, vm = meta
    nt = t // TILE_TOKENS
    halo_blocks = TILE_TOKENS // CONV_HALO
    kv_blocks = TILE_TOKENS // KV_HALO

    tile = pl.BlockSpec((None, TILE_TOKENS, D_MODEL), lambda bi, i: (bi, i, 0))
    u_prev = pl.BlockSpec((None, CONV_HALO, D_MODEL),
                          lambda bi, i: (bi, jnp.maximum(i * halo_blocks - 1, 0), 0))
    u_next = pl.BlockSpec((None, CONV_HALO, D_MODEL),
                          lambda bi, i: (bi, jnp.minimum((i + 1) * halo_blocks,
                                                         t // CONV_HALO - 1), 0))
    kv_prev = pl.BlockSpec((None, KV_HALO, D_MODEL),
                           lambda bi, i: (bi, jnp.maximum(i * kv_blocks - 1, 0), 0))
    kv_next = pl.BlockSpec((None, KV_HALO, D_MODEL),
                           lambda bi, i: (bi, jnp.minimum((i + 1) * kv_blocks,
                                                          t // KV_HALO - 1), 0))

    def const(shape):
        zeros = (0,) * len(shape)
        return pl.BlockSpec(shape, lambda bi, i: zeros, pipeline_mode=pl.Buffered(1))

    in_specs = [
        tile,
        tile, u_prev, u_next,
        tile, tile, tile,
        tile, kv_prev, kv_next,
        tile, kv_prev, kv_next,
        const((N_META, D_MODEL)), const((N_META, D_MODEL)), const((N_META, D_MODEL)),
        const((CONV_WIDTH, D_MODEL)), const((1, D_MODEL)), const((1, D_MODEL)),
        const((1, D_MODEL)),
        const(table.shape), const(head_mask.shape), const((1, D_MODEL)),
        const(w_out_bf.shape),
    ]
    return pl.pallas_call(
        functools.partial(_mixer_kernel, rows),
        grid=(b, nt),
        in_specs=in_specs,
        out_specs=tile,
        out_shape=jax.ShapeDtypeStruct(x.shape, x.dtype),
        scratch_shapes=[
            pltpu.VMEM((TILE_TOKENS + 2 * CONV_HALO, D_MODEL), F32),
            pltpu.VMEM((CONV_CHUNK, D_MODEL), F32),
            pltpu.VMEM((KV_WIN, D_MODEL), BF16),
            pltpu.VMEM((KV_WIN, D_MODEL), BF16),
            pltpu.VMEM((TILE_TOKENS, 2 * D_MODEL), BF16),
        ],
        compiler_params=pltpu.CompilerParams(
            dimension_semantics=("arbitrary", "arbitrary"),
            vmem_limit_bytes=VMEM_LIMIT),
        name="mixer",
    )(x, u, u, u, cg, q, ag, k, k, k, v, v, v, um, km, vm,
      conv_w, conv_b, ln_g, ln_b, table, head_mask, post_g, w_out_bf)


def _bias_table(rel_bias):
    qc = np.arange(GRID_W)[:, None]
    kc = np.arange(GRID_W)[None, :]
    start = np.clip(qc - WIN_COLS // 2, 0, GRID_W - WIN_COLS)
    valid = (kc >= start) & (kc < start + WIN_COLS)
    rel_idx = np.clip(kc - qc + WIN_COLS - 1, 0, 2 * WIN_COLS - 2)
    c = jnp.where(valid[None, None], rel_bias.astype(F32)[:, :, rel_idx], NEG_INF)
    return jnp.concatenate([c[:, :N_BIAS_ROWS], c[:, 1:N_BIAS_ROWS + 1]], axis=-1)


def _head_mask():
    r = np.arange(HEADS_PER_GROUP * GRID_W)[:, None] // GRID_W
    c = np.arange(GROUP_LANES)[None, :] // HEAD_DIM
    return jnp.asarray(r == c, BF16)


def kernel(x_prompt, x_sample, meta_tokens, pre_norm_g, w_in, conv_w, conv_b,
           conv_ln_g, conv_ln_b, rel_bias, post_norm_g, w_out):
    assert pre_norm_g.shape[0] == 1, "single-layer trunk"
    pre_g = pre_norm_g[0][None]
    w_in_bf = w_in[0].astype(BF16)
    w_out_bf = w_out[0].astype(BF16)
    table = _bias_table(rel_bias[0])
    head_mask = _head_mask()
    row = lambda a: a[0][None].astype(F32)

    meta_parts = _project(meta_tokens.astype(F32), pre_g, w_in_bf, N_META)
    meta = (meta_parts[0], meta_parts[3], meta_parts[4])

    outs = []
    for x in (x_prompt, x_sample):
        b, t, d = x.shape
        parts = _project(x.reshape(b * t, d), pre_g, w_in_bf, PROJ_TOKENS)
        parts = [p.reshape(b, t, d) for p in parts]
        outs.append(_mixer(x, parts, meta, conv_w[0].astype(F32), row(conv_b),
                           row(conv_ln_g), row(conv_ln_b), table, head_mask,
                           row(post_norm_g), w_out_bf))
    return tuple(outs)
```

```python
import functools

import numpy as np
import jax
import jax.numpy as jnp
from jax import lax
from jax.experimental import pallas as pl
from jax.experimental.pallas import tpu as pltpu

F32 = jnp.float32
BF16 = jnp.bfloat16

D_MODEL = 1024
N_META = 16
GRID_W = 64
WIN_ROWS = 8
WIN_COLS = 16
CONV_WIDTH = 31
CONV_PAD = CONV_WIDTH // 2
N_HEADS = 16
HEAD_DIM = 64
N_PARTS = 7
RMS_EPS = 1e-6
LN_EPS = 1e-5
NEG_INF = -1e30

LANES = 128
SUBLANES = 8
N_SLABS = D_MODEL // LANES

HEADS_PER_GROUP = 4
GROUP_LANES = HEADS_PER_GROUP * HEAD_DIM
N_GROUPS = N_HEADS // HEADS_PER_GROUP
ROW_PAIRS = WIN_ROWS // 2
N_BIAS_ROWS = 2 * WIN_ROWS - 2

PROJ_TOKENS = 512
TILE_ROWS = 8
TILE_TOKENS = TILE_ROWS * GRID_W
KV_HALO = 4 * GRID_W
KV_WIN = TILE_TOKENS + 2 * KV_HALO
CONV_HALO = 16
CONV_GROUP = 128
CONV_STRIDE = 4
CONV_SUB = SUBLANES * CONV_STRIDE

VMEM_LIMIT = 58 * 1024 * 1024


def _sigmoid(x):
    return 1.0 / (1.0 + jnp.exp(-x))


def _proj_kernel(x_ref, g_ref, w_ref, u_ref, cg_ref, q_ref, k_ref, v_ref, ag_ref):
    x = x_ref[...]
    ms = jnp.mean(x * x, axis=-1, keepdims=True)
    xn = (x * lax.rsqrt(ms + RMS_EPS) * g_ref[...]).astype(BF16)

    def part(p):
        return jnp.dot(xn, w_ref[:, p * D_MODEL:(p + 1) * D_MODEL],
                       preferred_element_type=F32)

    u_ref[...] = (part(0) * _sigmoid(part(1))).astype(BF16)
    z = part(2)
    cg_ref[...] = (z * _sigmoid(z)).astype(BF16)
    q_ref[...] = (part(3) * (HEAD_DIM ** -0.5)).astype(BF16)
    k_ref[...] = part(4).astype(BF16)
    v_ref[...] = part(5).astype(BF16)
    z = part(6)
    ag_ref[...] = (z * _sigmoid(z)).astype(BF16)


def _project(x2d, pre_g, w_in_bf, tm):
    n = x2d.shape[0]
    assert n % tm == 0
    out = jax.ShapeDtypeStruct((n, D_MODEL), BF16)
    tok = pl.BlockSpec((tm, D_MODEL), lambda i: (i, 0))
    const = lambda shape: pl.BlockSpec(shape, lambda i: (0, 0),
                                       pipeline_mode=pl.Buffered(1))
    return pl.pallas_call(
        _proj_kernel,
        grid=(n // tm,),
        in_specs=[tok, const((1, D_MODEL)), const((D_MODEL, N_PARTS * D_MODEL))],
        out_specs=[tok] * 6,
        out_shape=[out] * 6,
        compiler_params=pltpu.CompilerParams(
            dimension_semantics=("arbitrary",), vmem_limit_bytes=VMEM_LIMIT),
        name="in_proj",
    )(x2d, pre_g, w_in_bf)


def _mixer_kernel(rows,
                  x_ref, u_ref, up_ref, un_ref, cg_ref, q_ref, ag_ref,
                  k_ref, kp_ref, kn_ref, v_ref, vp_ref, vn_ref,
                  um_ref, km_ref, vm_ref, cw_ref, cb_ref, lg_ref, lb_ref,
                  tbl_ref, hm_ref, pg_ref, wo_ref,
                  o_ref,
                  ubuf, ybuf, kwin, vwin, mixed,
                  p_even, pm_even, rl_even, p_odd, pm_odd, rl_odd):
    i = pl.program_id(1)
    last = pl.num_programs(1) - 1

    def fill(row0, src_ref):
        for s in range(N_SLABS):
            ubuf[s, row0:row0 + src_ref.shape[0], :] = (
                src_ref[:, s * LANES:(s + 1) * LANES].astype(F32))

    fill(CONV_HALO, u_ref)

    @pl.when(i == 0)
    def _():
        fill(0, um_ref)

    @pl.when(i > 0)
    def _():
        fill(0, up_ref)

    @pl.when(i == last)
    def _():
        ubuf[:, CONV_HALO + TILE_TOKENS:, :] = jnp.zeros((N_SLABS, CONV_HALO, LANES), F32)

    @pl.when(i < last)
    def _():
        fill(CONV_HALO + TILE_TOKENS, un_ref)

    def conv_group(c, carry):
        base = pl.multiple_of(c * CONV_GROUP, CONV_GROUP)
        subs = [(sub, jj) for sub in range(CONV_GROUP // CONV_SUB) for jj in range(CONV_STRIDE)]
        for s in range(N_SLABS):
            ls = slice(s * LANES, (s + 1) * LANES)
            accs = [cb_ref[:, ls] for _ in subs]
            for t in range(CONV_WIDTH):
                w = cw_ref[t, :, ls]
                shift = CONV_HALO - CONV_PAD + t
                for a, (sub, jj) in enumerate(subs):
                    tap = ubuf[s, pl.ds(base + (sub * CONV_SUB + jj + shift), SUBLANES,
                                        stride=CONV_STRIDE), :]
                    accs[a] = accs[a] + tap * w
            for a, (sub, jj) in enumerate(subs):
                ybuf[s, pl.ds(sub * CONV_SUB + jj, SUBLANES, stride=CONV_STRIDE), :] = accs[a]
        y = jnp.concatenate([ybuf[s] for s in range(N_SLABS)], axis=1)
        mu = jnp.mean(y, axis=-1, keepdims=True)
        yc = y - mu
        var = jnp.mean(yc * yc, axis=-1, keepdims=True)
        yn = yc * lax.rsqrt(var + LN_EPS) * lg_ref[...] + lb_ref[...]
        act = yn * _sigmoid(yn)
        gate = cg_ref[pl.ds(base, CONV_GROUP), :].astype(F32)
        mixed[pl.ds(base, CONV_GROUP), 0:D_MODEL] = (act * gate).astype(BF16)
        return carry

    lax.fori_loop(0, TILE_TOKENS // CONV_GROUP, conv_group, 0)

    kwin[0:KV_HALO, :] = kp_ref[...]
    kwin[KV_HALO:KV_HALO + TILE_TOKENS, :] = k_ref[...]
    kwin[KV_HALO + TILE_TOKENS:, :] = kn_ref[...]
    vwin[0:KV_HALO, :] = vp_ref[...]
    vwin[KV_HALO:KV_HALO + TILE_TOKENS, :] = v_ref[...]
    vwin[KV_HALO + TILE_TOKENS:, :] = vn_ref[...]

    r0 = i * TILE_ROWS
    lane_head = lax.broadcasted_iota(jnp.int32, (GRID_W, GROUP_LANES), 1) // HEAD_DIM
    contract_last = (((1,), (1,)), ((), ()))

    def window(j):
        r = r0 + j
        sr = jnp.clip(r - WIN_ROWS // 2, 0, rows - WIN_ROWS)
        off = pl.multiple_of((sr - r0 + WIN_ROWS // 2) * GRID_W, GRID_W)
        return off, sr - r + (WIN_ROWS - 1)

    def scores(j, p_buf, pm_buf, rl_buf):
        off, d7 = window(j)
        qoff = pl.multiple_of(j * GRID_W, GRID_W)
        for g in range(N_GROUPS):
            ls = slice(g * GROUP_LANES, (g + 1) * GROUP_LANES)
            qg = q_ref[pl.ds(qoff, GRID_W), ls]
            qs = jnp.concatenate([qg] * HEADS_PER_GROUP, axis=0) * hm_ref[...]
            kt = kwin[pl.ds(off, WIN_ROWS * GRID_W), ls]
            s = lax.dot_general(qs, kt, contract_last, preferred_element_type=F32)
            bias = jnp.concatenate(
                [jnp.concatenate(
                    [tbl_ref[g * HEADS_PER_GROUP + hl, d7 + 2 * p] for p in range(ROW_PAIRS)],
                    axis=1) for hl in range(HEADS_PER_GROUP)], axis=0)
            s = s + bias
            sm = lax.dot_general(qs, km_ref[:, ls], contract_last,
                                 preferred_element_type=F32)
            m = jnp.maximum(jnp.max(s, axis=1, keepdims=True),
                            jnp.max(sm, axis=1, keepdims=True))
            p = jnp.exp(s - m)
            pm = jnp.exp(sm - m)
            l = jnp.sum(p, axis=1, keepdims=True) + jnp.sum(pm, axis=1, keepdims=True)
            p_buf[g] = p.astype(BF16)
            pm_buf[g] = pm.astype(BF16)
            rl_buf[g] = 1.0 / l

    def values(j, p_buf, pm_buf, rl_buf):
        off, _ = window(j)
        qoff = pl.multiple_of(j * GRID_W, GRID_W)
        for g in range(N_GROUPS):
            ls = slice(g * GROUP_LANES, (g + 1) * GROUP_LANES)
            vt = vwin[pl.ds(off, WIN_ROWS * GRID_W), ls]
            o = (jnp.dot(p_buf[g], vt, preferred_element_type=F32)
                 + jnp.dot(pm_buf[g], vm_ref[:, ls], preferred_element_type=F32))
            o = o * rl_buf[g]
            og = o[(HEADS_PER_GROUP - 1) * GRID_W:]
            for hl in range(HEADS_PER_GROUP - 2, -1, -1):
                og = jnp.where(lane_head == hl, o[hl * GRID_W:(hl + 1) * GRID_W], og)
            gate = ag_ref[pl.ds(qoff, GRID_W), ls].astype(F32)
            mixed[pl.ds(qoff, GRID_W), D_MODEL + g * GROUP_LANES:
                  D_MODEL + (g + 1) * GROUP_LANES] = (og * gate).astype(BF16)

    even = (p_even, pm_even, rl_even)
    odd = (p_odd, pm_odd, rl_odd)
    scores(0, *even)

    def row_pair(jj, carry):
        j = 2 * jj
        values(j, *even)
        scores(j + 1, *odd)
        values(j + 1, *odd)
        scores(j + 2, *even)
        return carry

    lax.fori_loop(0, TILE_ROWS // 2 - 1, row_pair, 0)
    values(TILE_ROWS - 2, *even)
    scores(TILE_ROWS - 1, *odd)
    values(TILE_ROWS - 1, *odd)

    y = jnp.dot(mixed[...], wo_ref[...], preferred_element_type=F32)
    ms = jnp.mean(y * y, axis=-1, keepdims=True)
    o_ref[...] = x_ref[...] + y * lax.rsqrt(ms + RMS_EPS) * pg_ref[...]


def _mixer(x, parts, meta, conv_w, conv_b, ln_g, ln_b, table, head_mask, post_g, w_out_bf):
    b, t, _ = x.shape
    rows = t // GRID_W
    assert t % TILE_TOKENS == 0 and rows >= WIN_ROWS
    u, cg, q, k, v, ag = parts
    um, km, vm = meta
    nt = t // TILE_TOKENS
    halo_blocks = TILE_TOKENS // CONV_HALO
    kv_blocks = TILE_TOKENS // KV_HALO

    tile = pl.BlockSpec((None, TILE_TOKENS, D_MODEL), lambda bi, i: (bi, i, 0))
    u_prev = pl.BlockSpec((None, CONV_HALO, D_MODEL),
                          lambda bi, i: (bi, jnp.maximum(i * halo_blocks - 1, 0), 0))
    u_next = pl.BlockSpec((None, CONV_HALO, D_MODEL),
                          lambda bi, i: (bi, jnp.minimum((i + 1) * halo_blocks,
                                                         t // CONV_HALO - 1), 0))
    kv_prev = pl.BlockSpec((None, KV_HALO, D_MODEL),
                           lambda bi, i: (bi, jnp.maximum(i * kv_blocks - 1, 0), 0))
    kv_next = pl.BlockSpec((None, KV_HALO, D_MODEL),
                           lambda bi, i: (bi, jnp.minimum((i + 1) * kv_blocks,
                                                          t // KV_HALO - 1), 0))

    def const(shape):
        zeros = (0,) * len(shape)
        return pl.BlockSpec(shape, lambda bi, i: zeros, pipeline_mode=pl.Buffered(1))

    in_specs = [
        tile,
        tile, u_prev, u_next,
        tile, tile, tile,
        tile, kv_prev, kv_next,
        tile, kv_prev, kv_next,
        const((N_META, D_MODEL)), const((N_META, D_MODEL)), const((N_META, D_MODEL)),
        const(conv_w.shape), const(conv_b.shape), const((1, D_MODEL)),
        const((1, D_MODEL)),
        const(table.shape), const(head_mask.shape), const((1, D_MODEL)),
        const(w_out_bf.shape),
    ]
    softmax_bufs = [
        pltpu.VMEM((N_GROUPS, HEADS_PER_GROUP * GRID_W, WIN_ROWS * GRID_W), BF16),
        pltpu.VMEM((N_GROUPS, HEADS_PER_GROUP * GRID_W, N_META), BF16),
        pltpu.VMEM((N_GROUPS, HEADS_PER_GROUP * GRID_W, 1), F32),
    ]
    return pl.pallas_call(
        functools.partial(_mixer_kernel, rows),
        grid=(b, nt),
        in_specs=in_specs,
        out_specs=tile,
        out_shape=jax.ShapeDtypeStruct(x.shape, x.dtype),
        scratch_shapes=[
            pltpu.VMEM((N_SLABS, TILE_TOKENS + 2 * CONV_HALO, LANES), F32),
            pltpu.VMEM((N_SLABS, CONV_GROUP, LANES), F32),
            pltpu.VMEM((KV_WIN, D_MODEL), BF16),
            pltpu.VMEM((KV_WIN, D_MODEL), BF16),
            pltpu.VMEM((TILE_TOKENS, 2 * D_MODEL), BF16),
        ] + softmax_bufs + softmax_bufs,
        compiler_params=pltpu.CompilerParams(
            dimension_semantics=("arbitrary", "arbitrary"),
            vmem_limit_bytes=VMEM_LIMIT),
        name="mixer",
    )(x, u, u, u, cg, q, ag, k, k, k, v, v, v, um, km, vm,
      conv_w, conv_b, ln_g, ln_b, table, head_mask, post_g, w_out_bf)


def _bias_table(rel_bias):
    qc = np.arange(GRID_W)[:, None]
    kc = np.arange(GRID_W)[None, :]
    start = np.clip(qc - WIN_COLS // 2, 0, GRID_W - WIN_COLS)
    valid = (kc >= start) & (kc < start + WIN_COLS)
    rel_idx = np.clip(kc - qc + WIN_COLS - 1, 0, 2 * WIN_COLS - 2)
    c = jnp.where(valid[None, None], rel_bias.astype(F32)[:, :, rel_idx], NEG_INF)
    return jnp.concatenate([c[:, :N_BIAS_ROWS], c[:, 1:N_BIAS_ROWS + 1]], axis=-1)


def _head_mask():
    r = np.arange(HEADS_PER_GROUP * GRID_W)[:, None] // GRID_W
    c = np.arange(GROUP_LANES)[None, :] // HEAD_DIM
    return jnp.asarray(r == c, BF16)


def _sublane_rows(a):
    return jnp.broadcast_to(a.astype(F32)[..., None, :], a.shape[:-1] + (SUBLANES, a.shape[-1]))


def kernel(x_prompt, x_sample, meta_tokens, pre_norm_g, w_in, conv_w, conv_b,
           conv_ln_g, conv_ln_b, rel_bias, post_norm_g, w_out):
    assert pre_norm_g.shape[0] == 1, "single-layer trunk"
    pre_g = pre_norm_g[0][None]
    w_in_bf = w_in[0].astype(BF16)
    w_out_bf = w_out[0].astype(BF16)
    table = _bias_table(rel_bias[0])
    head_mask = _head_mask()
    row = lambda a: a[0][None].astype(F32)
    conv_w8 = _sublane_rows(conv_w[0])
    conv_b8 = _sublane_rows(conv_b[0])

    meta_parts = _project(meta_tokens.astype(F32), pre_g, w_in_bf, N_META)
    meta = (meta_parts[0], meta_parts[3], meta_parts[4])

    outs = []
    for x in (x_prompt, x_sample):
        b, t, d = x.shape
        parts = _project(x.reshape(b * t, d), pre_g, w_in_bf, PROJ_TOKENS)
        parts = [p.reshape(b, t, d) for p in parts]
        outs.append(_mixer(x, parts, meta, conv_w8, conv_b8,
                           row(conv_ln_g), row(conv_ln_b), table, head_mask,
                           row(post_norm_g), w_out_bf))
    return tuple(outs)
```

```python
import functools

import numpy as np
import jax
import jax.numpy as jnp
from jax import lax
from jax.experimental import pallas as pl
from jax.experimental.pallas import tpu as pltpu

F32 = jnp.float32
BF16 = jnp.bfloat16

D_MODEL = 1024
N_META = 16
GRID_W = 64
WIN_ROWS = 8
WIN_COLS = 16
CONV_WIDTH = 31
CONV_PAD = CONV_WIDTH // 2
N_HEADS = 16
HEAD_DIM = 64
RMS_EPS = 1e-6
LN_EPS = 1e-5
NEG_INF = -1e30
LOG2_E = float(np.log2(np.e))
Q_SCALE = HEAD_DIM ** -0.5 * LOG2_E

LANES = 128
SUBLANES = 8
N_SLABS = D_MODEL // LANES

P_VAL, P_GLU, P_CGATE, P_Q, P_K, P_V, P_AGATE = range(7)
N_ROW_MAJOR = 5

PAIR_ROWS = 2
PAIR_TOKENS = PAIR_ROWS * GRID_W
PAIR_HEADS = 2
PAIR_LANES = PAIR_HEADS * HEAD_DIM
N_HEAD_PAIRS = N_HEADS // PAIR_HEADS
SCORE_LANES = PAIR_HEADS * PAIR_TOKENS
QUAD_PAIRS = 2
QUAD_HEADS = QUAD_PAIRS * PAIR_HEADS
QUAD_LANES = QUAD_HEADS * HEAD_DIM
QUAD_SCORE_LANES = QUAD_PAIRS * SCORE_LANES
KEY_ROWS = WIN_ROWS + PAIR_ROWS
KEY_TOKENS = KEY_ROWS * GRID_W
KEY_PAD = KEY_TOKENS + LANES
N_BIAS_ROWS = 2 * WIN_ROWS - 1

PROJ_TOKENS = 512
TILE_ROWS = 8
TILE_TOKENS = TILE_ROWS * GRID_W
KV_HALO = 4 * GRID_W
KV_WIN = TILE_TOKENS + 2 * KV_HALO
CONV_HALO = 16
CONV_GROUP = 128
CONV_STRIDE = 4
CONV_SUB = SUBLANES * CONV_STRIDE

VMEM_LIMIT = 58 * 1024 * 1024

CONTRACT_LAST = (((1,), (1,)), ((), ()))
CONTRACT_FIRST = (((0,), (0,)), ((), ()))


def _sigmoid(x):
    return 1.0 / (1.0 + jnp.exp(-x))


def _proj_kernel(x_ref, g_ref, w_ref, wt_ref, u_ref, cg_ref, q_ref, k_ref, vt_ref, agt_ref):
    x = x_ref[...]
    ms = jnp.mean(x * x, axis=-1, keepdims=True)
    xn = (x * lax.rsqrt(ms + RMS_EPS) * g_ref[...]).astype(BF16)

    def part(p):
        return jnp.dot(xn, w_ref[:, p * D_MODEL:(p + 1) * D_MODEL],
                       preferred_element_type=F32)

    def part_t(p):
        return lax.dot_general(wt_ref[p * D_MODEL:(p + 1) * D_MODEL, :], xn, CONTRACT_LAST,
                               preferred_element_type=F32)

    u_ref[...] = (part(P_VAL) * _sigmoid(part(P_GLU))).astype(BF16)
    z = part(P_CGATE)
    cg_ref[...] = (z * _sigmoid(z)).astype(BF16)
    q_ref[...] = (part(P_Q) * Q_SCALE).astype(BF16)
    k_ref[...] = part(P_K).astype(BF16)
    vt_ref[...] = part_t(0).astype(BF16)
    z = part_t(1)
    agt_ref[...] = (z * _sigmoid(z)).astype(BF16)


def _project(x2d, pre_g, w_rows, w_cols_t, tm):
    n = x2d.shape[0]
    assert n % tm == 0
    tok_major = jax.ShapeDtypeStruct((n, D_MODEL), BF16)
    feat_major = jax.ShapeDtypeStruct((D_MODEL, n), BF16)
    tok = pl.BlockSpec((tm, D_MODEL), lambda i: (i, 0))
    feat = pl.BlockSpec((D_MODEL, tm), lambda i: (0, i))
    const = lambda shape: pl.BlockSpec(shape, lambda i: (0, 0),
                                       pipeline_mode=pl.Buffered(1))
    return pl.pallas_call(
        _proj_kernel,
        grid=(n // tm,),
        in_specs=[tok, const((1, D_MODEL)), const(w_rows.shape), const(w_cols_t.shape)],
        out_specs=[tok] * 4 + [feat] * 2,
        out_shape=[tok_major] * 4 + [feat_major] * 2,
        compiler_params=pltpu.CompilerParams(
            dimension_semantics=("arbitrary",), vmem_limit_bytes=VMEM_LIMIT),
        name="in_proj",
    )(x2d, pre_g, w_rows, w_cols_t)


def _mixer_kernel(rows,
                  x_ref, u_ref, up_ref, un_ref, cg_ref, q_ref,
                  k_ref, kp_ref, kn_ref, vt_ref, vtp_ref, vtn_ref, agt_ref,
                  um_ref, km_ref, vmt_ref, cw_ref, cb_ref, lg_ref, lb_ref,
                  tbl_ref, pen_ref, hm_ref, pg_ref, w1_ref, w2_ref,
                  o_ref,
                  ubuf, ybuf, kwin, vtwin, conv_out, att_t, st_even, st_odd):
    i = pl.program_id(1)
    last = pl.num_programs(1) - 1

    def fill(row0, src_ref):
        for s in range(N_SLABS):
            ubuf[s, row0:row0 + src_ref.shape[0], :] = (
                src_ref[:, s * LANES:(s + 1) * LANES].astype(F32))

    fill(CONV_HALO, u_ref)

    @pl.when(i == 0)
    def _():
        fill(0, um_ref)

    @pl.when(i > 0)
    def _():
        fill(0, up_ref)

    @pl.when(i == last)
    def _():
        ubuf[:, CONV_HALO + TILE_TOKENS:, :] = jnp.zeros((N_SLABS, CONV_HALO, LANES), F32)

    @pl.when(i < last)
    def _():
        fill(CONV_HALO + TILE_TOKENS, un_ref)

    def conv_group(c, carry):
        base = pl.multiple_of(c * CONV_GROUP, CONV_GROUP)
        subs = [(sub, jj) for sub in range(CONV_GROUP // CONV_SUB) for jj in range(CONV_STRIDE)]
        for s in range(N_SLABS):
            ls = slice(s * LANES, (s + 1) * LANES)
            accs = [cb_ref[:, ls] for _ in subs]
            for t in range(CONV_WIDTH):
                w = cw_ref[t, :, ls]
                shift = CONV_HALO - CONV_PAD + t
                for a, (sub, jj) in enumerate(subs):
                    tap = ubuf[s, pl.ds(base + (sub * CONV_SUB + jj + shift), SUBLANES,
                                        stride=CONV_STRIDE), :]
                    accs[a] = accs[a] + tap * w
            for a, (sub, jj) in enumerate(subs):
                ybuf[s, pl.ds(sub * CONV_SUB + jj, SUBLANES, stride=CONV_STRIDE), :] = accs[a]
        y = jnp.concatenate([ybuf[s] for s in range(N_SLABS)], axis=1)
        mu = jnp.mean(y, axis=-1, keepdims=True)
        yc = y - mu
        var = jnp.mean(yc * yc, axis=-1, keepdims=True)
        yn = yc * lax.rsqrt(var + LN_EPS) * lg_ref[...] + lb_ref[...]
        act = yn * _sigmoid(yn)
        gate = cg_ref[pl.ds(base, CONV_GROUP), :].astype(F32)
        conv_out[pl.ds(base, CONV_GROUP), :] = (act * gate).astype(BF16)
        return carry

    lax.fori_loop(0, TILE_TOKENS // CONV_GROUP, conv_group, 0)

    kwin[0:KV_HALO, :] = kp_ref[...]
    kwin[KV_HALO:KV_HALO + TILE_TOKENS, :] = k_ref[...]
    kwin[KV_HALO + TILE_TOKENS:, :] = kn_ref[...]
    for blk in range(KV_WIN // LANES):
        tok0 = blk * LANES - KV_HALO
        if tok0 < 0:
            src = vtp_ref[:, tok0 + KV_HALO:tok0 + KV_HALO + LANES]
        elif tok0 < TILE_TOKENS:
            src = vt_ref[:, tok0:tok0 + LANES]
        else:
            src = vtn_ref[:, tok0 - TILE_TOKENS:tok0 - TILE_TOKENS + LANES]
        vtwin[blk] = src

    r0 = i * TILE_ROWS
    zero_keys = jnp.zeros((KEY_PAD - KEY_TOKENS - N_META, QUAD_SCORE_LANES), BF16)
    ones_rows = jnp.ones((2 * SUBLANES, KEY_PAD), BF16)

    def pair_geometry(jp):
        ra = r0 + jp * PAIR_ROWS
        sra = jnp.clip(ra - WIN_ROWS // 2, 0, rows - WIN_ROWS)
        srb = jnp.clip(ra + 1 - WIN_ROWS // 2, 0, rows - WIN_ROWS)
        win_row = sra - r0 + WIN_ROWS // 2
        shifted = (srb > sra).astype(jnp.int32)
        entries = [jnp.clip(sra + kr - ra + (WIN_ROWS - 1), 1, N_BIAS_ROWS) - 1
                   for kr in range(KEY_ROWS)]
        pen_case = {0: shifted, WIN_ROWS: 3 - shifted, WIN_ROWS + 1: 3}
        return dict(koff=pl.multiple_of(win_row * GRID_W, LANES), vblk=win_row // 2,
                    entries=entries, pen_case=pen_case,
                    tok=slice(jp * PAIR_TOKENS, (jp + 1) * PAIR_TOKENS))

    def score_matmul(geo, hq, st_ref):
        ls = slice(hq * QUAD_LANES, (hq + 1) * QUAD_LANES)
        q4 = q_ref[geo["tok"], ls]
        qs = jnp.concatenate([q4] * QUAD_HEADS, axis=0) * hm_ref[...]
        keys = jnp.concatenate([kwin[pl.ds(geo["koff"], KEY_TOKENS), ls], km_ref[:, ls]], axis=0)
        st_ref[...] = lax.dot_general(keys, qs, CONTRACT_LAST, preferred_element_type=F32)

    def softmax_values(geo, hq, st_ref):
        st = st_ref[...]
        blocks = []
        for kr in range(KEY_ROWS):
            e = geo["entries"][kr]
            b = jnp.concatenate([tbl_ref[hq * QUAD_HEADS + hh, e] for hh in range(QUAD_HEADS)],
                                axis=1)
            if kr in geo["pen_case"]:
                pen = pen_ref[geo["pen_case"][kr]]
                pen = jnp.concatenate([pen] * QUAD_HEADS, axis=1)
                b = b + jnp.concatenate([pen] * (GRID_W // SUBLANES), axis=0)
            blocks.append(b)
        blocks.append(jnp.zeros((N_META, QUAD_SCORE_LANES), F32))
        st = st + jnp.concatenate(blocks, axis=0)
        m = jnp.max(st, axis=0, keepdims=True)
        p = jnp.exp2(st - m)
        pt = jnp.concatenate([p.astype(BF16), zero_keys], axis=0)
        tok = geo["tok"]
        for hp in range(hq * QUAD_PAIRS, (hq + 1) * QUAD_PAIRS):
            vt = jnp.concatenate(
                [vtwin[geo["vblk"] + t, hp * PAIR_LANES:(hp + 1) * PAIR_LANES, :]
                 for t in range(KEY_TOKENS // LANES)]
                + [vmt_ref[hp * PAIR_LANES:(hp + 1) * PAIR_LANES, :]], axis=1)
            cols = slice((hp % QUAD_PAIRS) * SCORE_LANES, (hp % QUAD_PAIRS + 1) * SCORE_LANES)
            ot = jnp.dot(jnp.concatenate([vt, ones_rows], axis=0), pt[:, cols],
                         preferred_element_type=F32)
            ot = ot[:PAIR_LANES] * (1.0 / ot[PAIR_LANES:PAIR_LANES + 1])
            for hh in range(PAIR_HEADS):
                fs = slice(hp * PAIR_LANES + hh * HEAD_DIM, hp * PAIR_LANES + (hh + 1) * HEAD_DIM)
                blk = ot[hh * HEAD_DIM:(hh + 1) * HEAD_DIM,
                         hh * PAIR_TOKENS:(hh + 1) * PAIR_TOKENS]
                att_t[fs, tok] = (blk * agt_ref[fs, tok].astype(F32)).astype(BF16)

    geos = [pair_geometry(jp) for jp in range(TILE_ROWS // PAIR_ROWS)]
    steps = [(geo, hq) for geo in geos for hq in range(N_HEADS // QUAD_HEADS)]
    st_bufs = (st_even, st_odd)
    score_matmul(*steps[0], st_bufs[0])
    for n, step in enumerate(steps):
        if n + 1 < len(steps):
            score_matmul(*steps[n + 1], st_bufs[(n + 1) % 2])
        softmax_values(*step, st_bufs[n % 2])


    y = (jnp.dot(conv_out[...], w1_ref[...], preferred_element_type=F32)
         + lax.dot_general(att_t[...], w2_ref[...], CONTRACT_FIRST, preferred_element_type=F32))
    ms = jnp.mean(y * y, axis=-1, keepdims=True)
    o_ref[...] = x_ref[...] + y * lax.rsqrt(ms + RMS_EPS) * pg_ref[...]


def _mixer(x, parts, meta, consts):
    b, t, _ = x.shape
    rows = t // GRID_W
    assert t % TILE_TOKENS == 0 and rows >= WIN_ROWS and rows % 2 == 0
    u, cg, q, k, vt, agt = parts
    nt = t // TILE_TOKENS
    halo_blocks = TILE_TOKENS // CONV_HALO
    kv_blocks = TILE_TOKENS // KV_HALO

    tile = pl.BlockSpec((None, TILE_TOKENS, D_MODEL), lambda bi, i: (bi, i, 0))
    u_prev = pl.BlockSpec((None, CONV_HALO, D_MODEL),
                          lambda bi, i: (bi, jnp.maximum(i * halo_blocks - 1, 0), 0))
    u_next = pl.BlockSpec((None, CONV_HALO, D_MODEL),
                          lambda bi, i: (bi, jnp.minimum((i + 1) * halo_blocks,
                                                         t // CONV_HALO - 1), 0))
    k_prev = pl.BlockSpec((None, KV_HALO, D_MODEL),
                          lambda bi, i: (bi, jnp.maximum(i * kv_blocks - 1, 0), 0))
    k_next = pl.BlockSpec((None, KV_HALO, D_MODEL),
                          lambda bi, i: (bi, jnp.minimum((i + 1) * kv_blocks,
                                                         t // KV_HALO - 1), 0))
    tile_t = pl.BlockSpec((D_MODEL, TILE_TOKENS), lambda bi, i: (0, bi * nt + i))
    vt_prev = pl.BlockSpec((D_MODEL, KV_HALO),
                           lambda bi, i: (0, bi * nt * kv_blocks
                                          + jnp.maximum(i * kv_blocks - 1, 0)))
    vt_next = pl.BlockSpec((D_MODEL, KV_HALO),
                           lambda bi, i: (0, bi * nt * kv_blocks
                                          + jnp.minimum((i + 1) * kv_blocks, t // KV_HALO - 1)))

    def const(a):
        zeros = (0,) * a.ndim
        return pl.BlockSpec(a.shape, lambda bi, i: zeros, pipeline_mode=pl.Buffered(1))

    in_specs = [
        tile,
        tile, u_prev, u_next,
        tile, tile,
        tile, k_prev, k_next,
        tile_t, vt_prev, vt_next,
        tile_t,
    ] + [const(a) for a in meta + consts]
    return pl.pallas_call(
        functools.partial(_mixer_kernel, rows),
        grid=(b, nt),
        in_specs=in_specs,
        out_specs=tile,
        out_shape=jax.ShapeDtypeStruct(x.shape, x.dtype),
        scratch_shapes=[
            pltpu.VMEM((N_SLABS, TILE_TOKENS + 2 * CONV_HALO, LANES), F32),
            pltpu.VMEM((N_SLABS, CONV_GROUP, LANES), F32),
            pltpu.VMEM((KV_WIN, D_MODEL), BF16),
            pltpu.VMEM((KV_WIN // LANES, D_MODEL, LANES), BF16),
            pltpu.VMEM((TILE_TOKENS, D_MODEL), BF16),
            pltpu.VMEM((D_MODEL, TILE_TOKENS), BF16),
            pltpu.VMEM((KEY_TOKENS + N_META, QUAD_SCORE_LANES), F32),
            pltpu.VMEM((KEY_TOKENS + N_META, QUAD_SCORE_LANES), F32),
        ],
        compiler_params=pltpu.CompilerParams(
            dimension_semantics=("arbitrary", "arbitrary"),
            vmem_limit_bytes=VMEM_LIMIT),
        name="mixer",
    )(x, u, u, u, cg, q, k, k, k, vt, vt, vt, agt, *meta, *consts)


def _bias_table(rel_bias):
    kc = np.arange(GRID_W)[:, None]
    qc = np.arange(GRID_W)[None, :]
    start = np.clip(qc - WIN_COLS // 2, 0, GRID_W - WIN_COLS)
    valid = (kc >= start) & (kc < start + WIN_COLS)
    rel_idx = np.clip(kc - qc + WIN_COLS - 1, 0, 2 * WIN_COLS - 2)
    c = jnp.where(valid[None, None], rel_bias.astype(F32)[:, :, rel_idx], NEG_INF)
    c = jnp.concatenate([c, jnp.full_like(c[:, :1], NEG_INF)], axis=1)
    return jnp.concatenate([c[:, 1:], c[:, :-1]], axis=-1) * LOG2_E


def _row_penalties():
    pen = np.zeros((4, SUBLANES, LANES), np.float32)
    pen[1, :, GRID_W:] = NEG_INF
    pen[2, :, :GRID_W] = NEG_INF
    pen[3] = NEG_INF
    return jnp.asarray(pen)


def _head_mask():
    r = np.arange(QUAD_SCORE_LANES)[:, None] // PAIR_TOKENS
    c = np.arange(QUAD_LANES)[None, :] // HEAD_DIM
    return jnp.asarray(r == c, BF16)


def _sublane_rows(a):
    return jnp.broadcast_to(a.astype(F32)[..., None, :], a.shape[:-1] + (SUBLANES, a.shape[-1]))


def kernel(x_prompt, x_sample, meta_tokens, pre_norm_g, w_in, conv_w, conv_b,
           conv_ln_g, conv_ln_b, rel_bias, post_norm_g, w_out):
    assert pre_norm_g.shape[0] == 1, "single-layer trunk"
    pre_g = pre_norm_g[0][None]
    w_in_bf = w_in[0].astype(BF16)
    w_rows = w_in_bf[:, :N_ROW_MAJOR * D_MODEL]
    w_cols_t = w_in_bf[:, N_ROW_MAJOR * D_MODEL:].T
    w_out_bf = w_out[0].astype(BF16)
    row = lambda a: a[0][None].astype(F32)

    meta_parts = _project(meta_tokens.astype(F32), pre_g, w_rows, w_cols_t, N_META)
    vm_t = jnp.pad(meta_parts[4], ((0, 0), (0, LANES - N_META)))
    meta = [meta_parts[0], meta_parts[3], vm_t]
    consts = [_sublane_rows(conv_w[0]), _sublane_rows(conv_b[0]), row(conv_ln_g), row(conv_ln_b),
              _bias_table(rel_bias[0]), _row_penalties(), _head_mask(), row(post_norm_g),
              w_out_bf[:D_MODEL], w_out_bf[D_MODEL:]]

    outs = []
    for x in (x_prompt, x_sample):
        b, t, d = x.shape
        parts = _project(x.reshape(b * t, d), pre_g, w_rows, w_cols_t, PROJ_TOKENS)
        parts = [p.reshape(b, t, d) for p in parts[:4]] + list(parts[4:])
        outs.append(_mixer(x, parts, meta, consts))
    return tuple(outs)
```

```python
import functools

import numpy as np
import jax
import jax.numpy as jnp
from jax import lax
from jax.experimental import pallas as pl
from jax.experimental.pallas import tpu as pltpu

F32 = jnp.float32
BF16 = jnp.bfloat16

D_MODEL = 1024
N_META = 16
GRID_W = 64
WIN_ROWS = 8
WIN_COLS = 16
CONV_WIDTH = 31
CONV_PAD = CONV_WIDTH // 2
N_HEADS = 16
HEAD_DIM = 64
RMS_EPS = 1e-6
LN_EPS = 1e-5
NEG_INF = -1e30
LOG2_E = float(np.log2(np.e))
Q_SCALE = HEAD_DIM ** -0.5 * LOG2_E

LANES = 128
SUBLANES = 8
N_SLABS = D_MODEL // LANES

P_VAL, P_GLU, P_CGATE, P_Q, P_K, P_V, P_AGATE = range(7)
N_ROW_MAJOR = 5

PAIR_ROWS = 2
PAIR_TOKENS = PAIR_ROWS * GRID_W
PAIR_HEADS = 2
PAIR_LANES = PAIR_HEADS * HEAD_DIM
N_HEAD_PAIRS = N_HEADS // PAIR_HEADS
SCORE_LANES = PAIR_HEADS * PAIR_TOKENS
QUAD_PAIRS = 2
QUAD_HEADS = QUAD_PAIRS * PAIR_HEADS
QUAD_LANES = QUAD_HEADS * HEAD_DIM
QUAD_SCORE_LANES = QUAD_PAIRS * SCORE_LANES
KEY_ROWS = WIN_ROWS + PAIR_ROWS
KEY_TOKENS = KEY_ROWS * GRID_W
KEY_PAD = KEY_TOKENS + LANES
SCORE_ROWS = WIN_ROWS + PAIR_ROWS - 1
SCORE_TOKENS = SCORE_ROWS * GRID_W
N_BIAS_ROWS = 2 * WIN_ROWS - 1

PROJ_TOKENS = 512
TILE_ROWS = 8
TILE_TOKENS = TILE_ROWS * GRID_W
KV_HALO = 4 * GRID_W
KV_WIN = TILE_TOKENS + 2 * KV_HALO
CONV_HALO = 16
CONV_GROUP = 128
CONV_STRIDE = 4
CONV_SUB = SUBLANES * CONV_STRIDE

VMEM_LIMIT = 58 * 1024 * 1024

CONTRACT_LAST = (((1,), (1,)), ((), ()))
CONTRACT_FIRST = (((0,), (0,)), ((), ()))


def _sigmoid(x):
    return 1.0 / (1.0 + jnp.exp(-x))


def _proj_kernel(x_ref, g_ref, w_ref, wt_ref, u_ref, cg_ref, q_ref, k_ref, vt_ref, agt_ref):
    x = x_ref[...]
    ms = jnp.mean(x * x, axis=-1, keepdims=True)
    xn = (x * lax.rsqrt(ms + RMS_EPS) * g_ref[...]).astype(BF16)

    def part(p):
        return jnp.dot(xn, w_ref[:, p * D_MODEL:(p + 1) * D_MODEL],
                       preferred_element_type=F32)

    def part_t(p):
        return lax.dot_general(wt_ref[p * D_MODEL:(p + 1) * D_MODEL, :], xn, CONTRACT_LAST,
                               preferred_element_type=F32)

    u_ref[...] = (part(P_VAL) * _sigmoid(part(P_GLU))).astype(BF16)
    z = part(P_CGATE)
    cg_ref[...] = (z * _sigmoid(z)).astype(BF16)
    q_ref[...] = (part(P_Q) * Q_SCALE).astype(BF16)
    k_ref[...] = part(P_K).astype(BF16)
    vt_ref[...] = part_t(0).astype(BF16)
    z = part_t(1)
    agt_ref[...] = (z * _sigmoid(z)).astype(BF16)


def _project(x2d, pre_g, w_rows, w_cols_t, tm):
    n = x2d.shape[0]
    assert n % tm == 0
    tok_major = jax.ShapeDtypeStruct((n, D_MODEL), BF16)
    feat_major = jax.ShapeDtypeStruct((D_MODEL, n), BF16)
    tok = pl.BlockSpec((tm, D_MODEL), lambda i: (i, 0))
    feat = pl.BlockSpec((D_MODEL, tm), lambda i: (0, i))
    const = lambda shape: pl.BlockSpec(shape, lambda i: (0, 0),
                                       pipeline_mode=pl.Buffered(1))
    return pl.pallas_call(
        _proj_kernel,
        grid=(n // tm,),
        in_specs=[tok, const((1, D_MODEL)), const(w_rows.shape), const(w_cols_t.shape)],
        out_specs=[tok] * 4 + [feat] * 2,
        out_shape=[tok_major] * 4 + [feat_major] * 2,
        compiler_params=pltpu.CompilerParams(
            dimension_semantics=("arbitrary",), vmem_limit_bytes=VMEM_LIMIT),
        name="in_proj",
    )(x2d, pre_g, w_rows, w_cols_t)


def _mixer_kernel(rows,
                  x_ref, u_ref, up_ref, un_ref, cg_ref, q_ref,
                  k_ref, kp_ref, kn_ref, vt_ref, vtp_ref, vtn_ref, agt_ref,
                  um_ref, km_ref, vmt_ref, cw_ref, cb_ref, lg_ref, lb_ref,
                  tbl_ref, pen_ref, hm_ref, pg_ref, w1_ref, w2_ref,
                  o_ref,
                  ubuf, ybuf, kwin, vtwin, conv_out, att_t, st_even, st_odd):
    i = pl.program_id(1)
    last = pl.num_programs(1) - 1

    def fill(row0, src_ref):
        for s in range(N_SLABS):
            ubuf[s, row0:row0 + src_ref.shape[0], :] = (
                src_ref[:, s * LANES:(s + 1) * LANES].astype(F32))

    fill(CONV_HALO, u_ref)

    @pl.when(i == 0)
    def _():
        fill(0, um_ref)

    @pl.when(i > 0)
    def _():
        fill(0, up_ref)

    @pl.when(i == last)
    def _():
        ubuf[:, CONV_HALO + TILE_TOKENS:, :] = jnp.zeros((N_SLABS, CONV_HALO, LANES), F32)

    @pl.when(i < last)
    def _():
        fill(CONV_HALO + TILE_TOKENS, un_ref)

    def conv_group(c):
        base = c * CONV_GROUP
        subs = [(sub, jj) for sub in range(CONV_GROUP // CONV_SUB) for jj in range(CONV_STRIDE)]
        for s in range(N_SLABS):
            ls = slice(s * LANES, (s + 1) * LANES)
            accs = [cb_ref[:, ls] for _ in subs]
            for t in range(CONV_WIDTH):
                w = cw_ref[t, :, ls]
                shift = CONV_HALO - CONV_PAD + t
                for a, (sub, jj) in enumerate(subs):
                    tap = ubuf[s, pl.ds(base + (sub * CONV_SUB + jj + shift), SUBLANES,
                                        stride=CONV_STRIDE), :]
                    accs[a] = accs[a] + tap * w
            for a, (sub, jj) in enumerate(subs):
                ybuf[s, pl.ds(sub * CONV_SUB + jj, SUBLANES, stride=CONV_STRIDE), :] = accs[a]
        y = jnp.concatenate([ybuf[s] for s in range(N_SLABS)], axis=1)
        mu = jnp.mean(y, axis=-1, keepdims=True)
        yc = y - mu
        var = jnp.mean(yc * yc, axis=-1, keepdims=True)
        yn = yc * lax.rsqrt(var + LN_EPS) * lg_ref[...] + lb_ref[...]
        act = yn * _sigmoid(yn)
        gate = cg_ref[pl.ds(base, CONV_GROUP), :].astype(F32)
        conv_out[pl.ds(base, CONV_GROUP), :] = (act * gate).astype(BF16)

    kwin[0:KV_HALO, :] = kp_ref[...]
    kwin[KV_HALO:KV_HALO + TILE_TOKENS, :] = k_ref[...]
    kwin[KV_HALO + TILE_TOKENS:, :] = kn_ref[...]
    for blk in range(KV_WIN // LANES):
        tok0 = blk * LANES - KV_HALO
        if tok0 < 0:
            src = vtp_ref[:, tok0 + KV_HALO:tok0 + KV_HALO + LANES]
        elif tok0 < TILE_TOKENS:
            src = vt_ref[:, tok0:tok0 + LANES]
        else:
            src = vtn_ref[:, tok0 - TILE_TOKENS:tok0 - TILE_TOKENS + LANES]
        vtwin[blk] = src

    r0 = i * TILE_ROWS
    zero_row = jnp.zeros((KEY_TOKENS - SCORE_TOKENS, QUAD_SCORE_LANES), BF16)
    zero_keys = jnp.zeros((KEY_PAD - KEY_TOKENS - N_META, QUAD_SCORE_LANES), BF16)
    ones_rows = jnp.ones((2 * SUBLANES, KEY_PAD), BF16)

    def pair_geometry(jp):
        ra = r0 + jp * PAIR_ROWS
        sra = jnp.clip(ra - WIN_ROWS // 2, 0, rows - WIN_ROWS)
        srb = jnp.clip(ra + 1 - WIN_ROWS // 2, 0, rows - WIN_ROWS)
        win_row = sra - r0 + WIN_ROWS // 2
        shifted = (srb > sra).astype(jnp.int32)
        entries = [jnp.clip(sra + kr - ra + (WIN_ROWS - 1), 1, N_BIAS_ROWS) - 1
                   for kr in range(SCORE_ROWS)]
        pen_case = {0: shifted, WIN_ROWS: 3 - shifted}
        return dict(koff=pl.multiple_of(win_row * GRID_W, LANES), vblk=win_row // 2,
                    entries=entries, pen_case=pen_case,
                    tok=slice(jp * PAIR_TOKENS, (jp + 1) * PAIR_TOKENS))

    def score_matmul(geo, hq, st_ref):
        ls = slice(hq * QUAD_LANES, (hq + 1) * QUAD_LANES)
        q4 = q_ref[geo["tok"], ls]
        qs = jnp.concatenate([q4] * QUAD_HEADS, axis=0) * hm_ref[...]
        keys = jnp.concatenate([kwin[pl.ds(geo["koff"], SCORE_TOKENS), ls], km_ref[:, ls]], axis=0)
        st_ref[...] = lax.dot_general(keys, qs, CONTRACT_LAST, preferred_element_type=F32)

    def softmax_values(geo, hq, st_ref):
        st = st_ref[...]
        blocks = []
        for kr in range(SCORE_ROWS):
            e = geo["entries"][kr]
            b = jnp.concatenate([tbl_ref[hq * QUAD_HEADS + hh, e] for hh in range(QUAD_HEADS)],
                                axis=1)
            if kr in geo["pen_case"]:
                pen = pen_ref[geo["pen_case"][kr]]
                pen = jnp.concatenate([pen] * QUAD_HEADS, axis=1)
                b = b + jnp.concatenate([pen] * (GRID_W // SUBLANES), axis=0)
            blocks.append(b)
        blocks.append(jnp.zeros((N_META, QUAD_SCORE_LANES), F32))
        st = st + jnp.concatenate(blocks, axis=0)
        m = jnp.max(st, axis=0, keepdims=True)
        p = jnp.exp2(st - m)
        pb = p.astype(BF16)
        pt = jnp.concatenate([pb[:SCORE_TOKENS], zero_row, pb[SCORE_TOKENS:], zero_keys],
                             axis=0)
        tok = geo["tok"]
        for hp in range(hq * QUAD_PAIRS, (hq + 1) * QUAD_PAIRS):
            vt = jnp.concatenate(
                [vtwin[geo["vblk"] + t, hp * PAIR_LANES:(hp + 1) * PAIR_LANES, :]
                 for t in range(KEY_TOKENS // LANES)]
                + [vmt_ref[hp * PAIR_LANES:(hp + 1) * PAIR_LANES, :]], axis=1)
            cols = slice((hp % QUAD_PAIRS) * SCORE_LANES, (hp % QUAD_PAIRS + 1) * SCORE_LANES)
            ot = jnp.dot(jnp.concatenate([vt, ones_rows], axis=0), pt[:, cols],
                         preferred_element_type=F32)
            ot = ot[:PAIR_LANES] * (1.0 / ot[PAIR_LANES:PAIR_LANES + 1])
            for hh in range(PAIR_HEADS):
                fs = slice(hp * PAIR_LANES + hh * HEAD_DIM, hp * PAIR_LANES + (hh + 1) * HEAD_DIM)
                blk = ot[hh * HEAD_DIM:(hh + 1) * HEAD_DIM,
                         hh * PAIR_TOKENS:(hh + 1) * PAIR_TOKENS]
                att_t[fs, tok] = (blk * agt_ref[fs, tok].astype(F32)).astype(BF16)

    geos = [pair_geometry(jp) for jp in range(TILE_ROWS // PAIR_ROWS)]
    steps = [(geo, hq) for geo in geos for hq in range(N_HEADS // QUAD_HEADS)]
    st_bufs = (st_even, st_odd)
    n_conv = TILE_TOKENS // CONV_GROUP
    score_matmul(*steps[0], st_bufs[0])
    for n, step in enumerate(steps):
        if n % (len(steps) // n_conv) == 0:
            conv_group(n // (len(steps) // n_conv))
        if n + 1 < len(steps):
            score_matmul(*steps[n + 1], st_bufs[(n + 1) % 2])
        softmax_values(*step, st_bufs[n % 2])


    y = (jnp.dot(conv_out[...], w1_ref[...], preferred_element_type=F32)
         + lax.dot_general(att_t[...], w2_ref[...], CONTRACT_FIRST, preferred_element_type=F32))
    ms = jnp.mean(y * y, axis=-1, keepdims=True)
    o_ref[...] = x_ref[...] + y * lax.rsqrt(ms + RMS_EPS) * pg_ref[...]


def _mixer(x, parts, meta, consts):
    b, t, _ = x.shape
    rows = t // GRID_W
    assert t % TILE_TOKENS == 0 and rows >= WIN_ROWS and rows % 2 == 0
    u, cg, q, k, vt, agt = parts
    nt = t // TILE_TOKENS
    halo_blocks = TILE_TOKENS // CONV_HALO
    kv_blocks = TILE_TOKENS // KV_HALO

    tile = pl.BlockSpec((None, TILE_TOKENS, D_MODEL), lambda bi, i: (bi, i, 0))
    u_prev = pl.BlockSpec((None, CONV_HALO, D_MODEL),
                          lambda bi, i: (bi, jnp.maximum(i * halo_blocks - 1, 0), 0))
    u_next = pl.BlockSpec((None, CONV_HALO, D_MODEL),
                          lambda bi, i: (bi, jnp.minimum((i + 1) * halo_blocks,
                                                         t // CONV_HALO - 1), 0))
    k_prev = pl.BlockSpec((None, KV_HALO, D_MODEL),
                          lambda bi, i: (bi, jnp.maximum(i * kv_blocks - 1, 0), 0))
    k_next = pl.BlockSpec((None, KV_HALO, D_MODEL),
                          lambda bi, i: (bi, jnp.minimum((i + 1) * kv_blocks,
                                                         t // KV_HALO - 1), 0))
    tile_t = pl.BlockSpec((D_MODEL, TILE_TOKENS), lambda bi, i: (0, bi * nt + i))
    vt_prev = pl.BlockSpec((D_MODEL, KV_HALO),
                           lambda bi, i: (0, bi * nt * kv_blocks
                                          + jnp.maximum(i * kv_blocks - 1, 0)))
    vt_next = pl.BlockSpec((D_MODEL, KV_HALO),
                           lambda bi, i: (0, bi * nt * kv_blocks
                                          + jnp.minimum((i + 1) * kv_blocks, t // KV_HALO - 1)))

    def const(a):
        zeros = (0,) * a.ndim
        return pl.BlockSpec(a.shape, lambda bi, i: zeros, pipeline_mode=pl.Buffered(1))

    in_specs = [
        tile,
        tile, u_prev, u_next,
        tile, tile,
        tile, k_prev, k_next,
        tile_t, vt_prev, vt_next,
        tile_t,
    ] + [const(a) for a in meta + consts]
    return pl.pallas_call(
        functools.partial(_mixer_kernel, rows),
        grid=(b, nt),
        in_specs=in_specs,
        out_specs=tile,
        out_shape=jax.ShapeDtypeStruct(x.shape, x.dtype),
        scratch_shapes=[
            pltpu.VMEM((N_SLABS, TILE_TOKENS + 2 * CONV_HALO, LANES), F32),
            pltpu.VMEM((N_SLABS, CONV_GROUP, LANES), F32),
            pltpu.VMEM((KV_WIN, D_MODEL), BF16),
            pltpu.VMEM((KV_WIN // LANES, D_MODEL, LANES), BF16),
            pltpu.VMEM((TILE_TOKENS, D_MODEL), BF16),
            pltpu.VMEM((D_MODEL, TILE_TOKENS), BF16),
            pltpu.VMEM((SCORE_TOKENS + N_META, QUAD_SCORE_LANES), F32),
            pltpu.VMEM((SCORE_TOKENS + N_META, QUAD_SCORE_LANES), F32),
        ],
        compiler_params=pltpu.CompilerParams(
            dimension_semantics=("arbitrary", "arbitrary"),
            vmem_limit_bytes=VMEM_LIMIT),
        name="mixer",
    )(x, u, u, u, cg, q, k, k, k, vt, vt, vt, agt, *meta, *consts)


def _bias_table(rel_bias):
    kc = np.arange(GRID_W)[:, None]
    qc = np.arange(GRID_W)[None, :]
    start = np.clip(qc - WIN_COLS // 2, 0, GRID_W - WIN_COLS)
    valid = (kc >= start) & (kc < start + WIN_COLS)
    rel_idx = np.clip(kc - qc + WIN_COLS - 1, 0, 2 * WIN_COLS - 2)
    c = jnp.where(valid[None, None], rel_bias.astype(F32)[:, :, rel_idx], NEG_INF)
    c = jnp.concatenate([c, jnp.full_like(c[:, :1], NEG_INF)], axis=1)
    return jnp.concatenate([c[:, 1:], c[:, :-1]], axis=-1) * LOG2_E


def _row_penalties():
    pen = np.zeros((4, SUBLANES, LANES), np.float32)
    pen[1, :, GRID_W:] = NEG_INF
    pen[2, :, :GRID_W] = NEG_INF
    pen[3] = NEG_INF
    return jnp.asarray(pen)


def _head_mask():
    r = np.arange(QUAD_SCORE_LANES)[:, None] // PAIR_TOKENS
    c = np.arange(QUAD_LANES)[None, :] // HEAD_DIM
    return jnp.asarray(r == c, BF16)


def _sublane_rows(a):
    return jnp.broadcast_to(a.astype(F32)[..., None, :], a.shape[:-1] + (SUBLANES, a.shape[-1]))


def kernel(x_prompt, x_sample, meta_tokens, pre_norm_g, w_in, conv_w, conv_b,
           conv_ln_g, conv_ln_b, rel_bias, post_norm_g, w_out):
    assert pre_norm_g.shape[0] == 1, "single-layer trunk"
    pre_g = pre_norm_g[0][None]
    w_in_bf = w_in[0].astype(BF16)
    w_rows = w_in_bf[:, :N_ROW_MAJOR * D_MODEL]
    w_cols_t = w_in_bf[:, N_ROW_MAJOR * D_MODEL:].T
    w_out_bf = w_out[0].astype(BF16)
    row = lambda a: a[0][None].astype(F32)

    meta_parts = _project(meta_tokens.astype(F32), pre_g, w_rows, w_cols_t, N_META)
    vm_t = jnp.pad(meta_parts[4], ((0, 0), (0, LANES - N_META)))
    meta = [meta_parts[0], meta_parts[3], vm_t]
    consts = [_sublane_rows(conv_w[0]), _sublane_rows(conv_b[0]), row(conv_ln_g), row(conv_ln_b),
              _bias_table(rel_bias[0]), _row_penalties(), _head_mask(), row(post_norm_g),
              w_out_bf[:D_MODEL], w_out_bf[D_MODEL:]]

    outs = []
    for x in (x_prompt, x_sample):
        b, t, d = x.shape
        parts = _project(x.reshape(b * t, d), pre_g, w_rows, w_cols_t, PROJ_TOKENS)
        parts = [p.reshape(b, t, d) for p in parts[:4]] + list(parts[4:])
        outs.append(_mixer(x, parts, meta, consts))
    return tuple(outs)
```

```python
import functools

import numpy as np
import jax
import jax.numpy as jnp
from jax import lax
from jax.experimental import pallas as pl
from jax.experimental.pallas import tpu as pltpu

F32 = jnp.float32
BF16 = jnp.bfloat16

D_MODEL = 1024
N_META = 16
GRID_W = 64
WIN_ROWS = 8
WIN_COLS = 16
CONV_WIDTH = 31
CONV_PAD = CONV_WIDTH // 2
N_HEADS = 16
HEAD_DIM = 64
RMS_EPS = 1e-6
LN_EPS = 1e-5
NEG_INF = -1e30
LOG2_E = float(np.log2(np.e))
Q_SCALE = HEAD_DIM ** -0.5 * LOG2_E

LANES = 128
SUBLANES = 8
N_SLABS = D_MODEL // LANES

P_VAL, P_GLU, P_CGATE, P_Q, P_K, P_V, P_AGATE = range(7)
N_ROW_MAJOR = 5

PAIR_ROWS = 2
PAIR_TOKENS = PAIR_ROWS * GRID_W
PAIR_HEADS = 2
PAIR_LANES = PAIR_HEADS * HEAD_DIM
SCORE_LANES = PAIR_HEADS * PAIR_TOKENS
QUAD_PAIRS = 2
QUAD_HEADS = QUAD_PAIRS * PAIR_HEADS
QUAD_LANES = QUAD_HEADS * HEAD_DIM
QUAD_SCORE_LANES = QUAD_PAIRS * SCORE_LANES
KEY_ROWS = WIN_ROWS + PAIR_ROWS
KEY_TOKENS = KEY_ROWS * GRID_W
KEY_PAD = KEY_TOKENS + LANES
SCORE_ROWS = WIN_ROWS + PAIR_ROWS - 1
SCORE_TOKENS = SCORE_ROWS * GRID_W
N_BIAS_ROWS = 2 * WIN_ROWS - 1

PROJ_TOKENS = 512
TILE_ROWS = 8
TILE_TOKENS = TILE_ROWS * GRID_W
KV_HALO = 4 * GRID_W
KV_WIN = TILE_TOKENS + 2 * KV_HALO
CONV_HALO = 16
CONV_GROUP = 64
CONV_STRIDE = 4
CONV_SUB = SUBLANES * CONV_STRIDE

VMEM_LIMIT = 58 * 1024 * 1024

CONTRACT_LAST = (((1,), (1,)), ((), ()))
CONTRACT_FIRST = (((0,), (0,)), ((), ()))


def _sigmoid(x):
    return 1.0 / (1.0 + jnp.exp(-x))


def _exact_zero(x):
    bits = lax.shift_right_logical(pltpu.bitcast(x, jnp.uint32), jnp.uint32(32))
    return pltpu.bitcast(bits, F32)


def _proj_kernel(x_ref, g_ref, w_ref, wt_ref, u_ref, cg_ref, q_ref, k_ref, vt_ref, agt_ref):
    x = x_ref[...]
    ms = jnp.mean(x * x, axis=-1, keepdims=True)
    xn = (x * lax.rsqrt(ms + RMS_EPS) * g_ref[...]).astype(BF16)

    def part(p):
        return jnp.dot(xn, w_ref[:, p * D_MODEL:(p + 1) * D_MODEL],
                       preferred_element_type=F32)

    def part_t(p):
        return lax.dot_general(wt_ref[p * D_MODEL:(p + 1) * D_MODEL, :], xn, CONTRACT_LAST,
                               preferred_element_type=F32)

    u_ref[...] = (part(P_VAL) * _sigmoid(part(P_GLU))).astype(BF16)
    z = part(P_CGATE)
    cg_ref[...] = (z * _sigmoid(z)).astype(BF16)
    q_ref[...] = (part(P_Q) * Q_SCALE).astype(BF16)
    k_ref[...] = part(P_K).astype(BF16)
    vt_ref[...] = part_t(0).astype(BF16)
    z = part_t(1)
    agt_ref[...] = (z * _sigmoid(z)).astype(BF16)


def _project(x2d, pre_g, w_rows, w_cols_t, tm):
    n = x2d.shape[0]
    assert n % tm == 0
    tok_major = jax.ShapeDtypeStruct((n, D_MODEL), BF16)
    feat_major = jax.ShapeDtypeStruct((D_MODEL, n), BF16)
    tok = pl.BlockSpec((tm, D_MODEL), lambda i: (i, 0))
    feat = pl.BlockSpec((D_MODEL, tm), lambda i: (0, i))
    const = lambda shape: pl.BlockSpec(shape, lambda i: (0, 0),
                                       pipeline_mode=pl.Buffered(1))
    return pl.pallas_call(
        _proj_kernel,
        grid=(n // tm,),
        in_specs=[tok, const((1, D_MODEL)), const(w_rows.shape), const(w_cols_t.shape)],
        out_specs=[tok] * 4 + [feat] * 2,
        out_shape=[tok_major] * 4 + [feat_major] * 2,
        compiler_params=pltpu.CompilerParams(
            dimension_semantics=("arbitrary",), vmem_limit_bytes=VMEM_LIMIT),
        name="in_proj",
    )(x2d, pre_g, w_rows, w_cols_t)


def _mixer_kernel(rows,
                  x_ref, u_ref, up_ref, un_ref, cg_ref, q_ref,
                  k_ref, kp_ref, kn_ref, vt_ref, vtp_ref, vtn_ref, agt_ref,
                  um_ref, km_ref, vmt_ref, cw_ref, cb_ref, lg_ref, lb_ref,
                  tbl_ref, pen_ref, hm_ref, pg_ref, w1_ref, w2_ref,
                  o_ref,
                  ubuf, ybuf, kwin, vtwin, conv_out, att_t, st_even, st_odd):
    i = pl.program_id(1)
    last = pl.num_programs(1) - 1

    def fill(row0, src_ref):
        for s in range(N_SLABS):
            ubuf[s, row0:row0 + src_ref.shape[0], :] = (
                src_ref[:, s * LANES:(s + 1) * LANES].astype(F32))

    fill(CONV_HALO, u_ref)

    @pl.when(i == 0)
    def _():
        fill(0, um_ref)

    @pl.when(i > 0)
    def _():
        fill(0, up_ref)

    @pl.when(i == last)
    def _():
        ubuf[:, CONV_HALO + TILE_TOKENS:, :] = jnp.zeros((N_SLABS, CONV_HALO, LANES), F32)

    @pl.when(i < last)
    def _():
        fill(CONV_HALO + TILE_TOKENS, un_ref)

    def conv_group(c):
        base = c * CONV_GROUP
        subs = [(sub, jj) for sub in range(CONV_GROUP // CONV_SUB) for jj in range(CONV_STRIDE)]
        for s in range(N_SLABS):
            ls = slice(s * LANES, (s + 1) * LANES)
            accs = [cb_ref[:, ls] for _ in subs]
            for t in range(CONV_WIDTH):
                w = cw_ref[t, :, ls]
                shift = CONV_HALO - CONV_PAD + t
                for a, (sub, jj) in enumerate(subs):
                    tap = ubuf[s, pl.ds(base + (sub * CONV_SUB + jj + shift), SUBLANES,
                                        stride=CONV_STRIDE), :]
                    accs[a] = accs[a] + tap * w
            for a, (sub, jj) in enumerate(subs):
                ybuf[s, pl.ds(sub * CONV_SUB + jj, SUBLANES, stride=CONV_STRIDE), :] = accs[a]
        y = jnp.concatenate([ybuf[s] for s in range(N_SLABS)], axis=1)
        mu = jnp.mean(y, axis=-1, keepdims=True)
        yc = y - mu
        var = jnp.mean(yc * yc, axis=-1, keepdims=True)
        yn = yc * lax.rsqrt(var + LN_EPS) * lg_ref[...] + lb_ref[...]
        act = yn * _sigmoid(yn)
        gate = cg_ref[pl.ds(base, CONV_GROUP), :].astype(F32)
        gated = act * gate
        conv_out[pl.ds(base, CONV_GROUP), :] = gated.astype(BF16)
        probe = gated[0:SUBLANES, 0:LANES]
        for blk in range(1, CONV_GROUP // SUBLANES):
            probe = probe + gated[blk * SUBLANES:(blk + 1) * SUBLANES, 0:LANES]
        return _exact_zero(probe)

    kwin[0:KV_HALO, :] = kp_ref[...]
    kwin[KV_HALO:KV_HALO + TILE_TOKENS, :] = k_ref[...]
    kwin[KV_HALO + TILE_TOKENS:, :] = kn_ref[...]
    for blk in range(KV_WIN // LANES):
        tok0 = blk * LANES - KV_HALO
        if tok0 < 0:
            src = vtp_ref[:, tok0 + KV_HALO:tok0 + KV_HALO + LANES]
        elif tok0 < TILE_TOKENS:
            src = vt_ref[:, tok0:tok0 + LANES]
        else:
            src = vtn_ref[:, tok0 - TILE_TOKENS:tok0 - TILE_TOKENS + LANES]
        vtwin[blk] = src

    r0 = i * TILE_ROWS
    zero_row = jnp.zeros((KEY_TOKENS - SCORE_TOKENS, QUAD_SCORE_LANES), BF16)
    zero_keys = jnp.zeros((KEY_PAD - KEY_TOKENS - N_META, QUAD_SCORE_LANES), BF16)
    ones_rows = jnp.ones((2 * SUBLANES, KEY_PAD), BF16)

    def pair_geometry(jp):
        ra = r0 + jp * PAIR_ROWS
        sra = jnp.clip(ra - WIN_ROWS // 2, 0, rows - WIN_ROWS)
        srb = jnp.clip(ra + 1 - WIN_ROWS // 2, 0, rows - WIN_ROWS)
        win_row = sra - r0 + WIN_ROWS // 2
        shifted = (srb > sra).astype(jnp.int32)
        entries = [jnp.clip(sra + kr - ra + (WIN_ROWS - 1), 1, N_BIAS_ROWS) - 1
                   for kr in range(SCORE_ROWS)]
        pen_case = {0: shifted, WIN_ROWS: 3 - shifted}
        return dict(koff=pl.multiple_of(win_row * GRID_W, LANES), vblk=win_row // 2,
                    entries=entries, pen_case=pen_case,
                    tok=slice(jp * PAIR_TOKENS, (jp + 1) * PAIR_TOKENS))

    def score_matmul(geo, hq, st_ref):
        ls = slice(hq * QUAD_LANES, (hq + 1) * QUAD_LANES)
        q4 = q_ref[geo["tok"], ls]
        qs = jnp.concatenate([q4] * QUAD_HEADS, axis=0) * hm_ref[...]
        keys = jnp.concatenate([kwin[pl.ds(geo["koff"], SCORE_TOKENS), ls], km_ref[:, ls]], axis=0)
        st_ref[...] = lax.dot_general(keys, qs, CONTRACT_LAST, preferred_element_type=F32)

    def softmax_values(geo, hq, st_ref):
        st = st_ref[...]
        blocks = []
        for kr in range(SCORE_ROWS):
            e = geo["entries"][kr]
            b = jnp.concatenate([tbl_ref[hq * QUAD_HEADS + hh, e] for hh in range(QUAD_HEADS)],
                                axis=1)
            if kr in geo["pen_case"]:
                pen = pen_ref[geo["pen_case"][kr]]
                pen = jnp.concatenate([pen] * QUAD_HEADS, axis=1)
                b = b + jnp.concatenate([pen] * (GRID_W // SUBLANES), axis=0)
            blocks.append(b)
        blocks.append(jnp.zeros((N_META, QUAD_SCORE_LANES), F32))
        st = st + jnp.concatenate(blocks, axis=0)
        m = jnp.max(st, axis=0, keepdims=True)
        p = jnp.exp2(st - m)
        pb = p.astype(BF16)
        pt = jnp.concatenate([pb[:SCORE_TOKENS], zero_row, pb[SCORE_TOKENS:], zero_keys],
                             axis=0)
        tok = geo["tok"]
        for hp in range(hq * QUAD_PAIRS, (hq + 1) * QUAD_PAIRS):
            vt = jnp.concatenate(
                [vtwin[geo["vblk"] + t, hp * PAIR_LANES:(hp + 1) * PAIR_LANES, :]
                 for t in range(KEY_TOKENS // LANES)]
                + [vmt_ref[hp * PAIR_LANES:(hp + 1) * PAIR_LANES, :]], axis=1)
            cols = slice((hp % QUAD_PAIRS) * SCORE_LANES, (hp % QUAD_PAIRS + 1) * SCORE_LANES)
            ot = jnp.dot(jnp.concatenate([vt, ones_rows], axis=0), pt[:, cols],
                         preferred_element_type=F32)
            ot = ot[:PAIR_LANES] * (1.0 / ot[PAIR_LANES:PAIR_LANES + 1])
            for hh in range(PAIR_HEADS):
                fs = slice(hp * PAIR_LANES + hh * HEAD_DIM, hp * PAIR_LANES + (hh + 1) * HEAD_DIM)
                blk = ot[hh * HEAD_DIM:(hh + 1) * HEAD_DIM,
                         hh * PAIR_TOKENS:(hh + 1) * PAIR_TOKENS]
                att_t[fs, tok] = (blk * agt_ref[fs, tok].astype(F32)).astype(BF16)

    geos = [pair_geometry(jp) for jp in range(TILE_ROWS // PAIR_ROWS)]
    steps = [(geo, hq) for geo in geos for hq in range(N_HEADS // QUAD_HEADS)]
    st_bufs = (st_even, st_odd)
    steps_per_conv = len(steps) * CONV_GROUP // TILE_TOKENS
    anchors = {}
    score_matmul(*steps[0], st_bufs[0])
    for n, step in enumerate(steps):
        if n % steps_per_conv == 0:
            anchors[n + steps_per_conv] = conv_group(n // steps_per_conv)
        if n + 1 < len(steps):
            nxt = st_bufs[(n + 1) % 2]
            score_matmul(*steps[n + 1], nxt)
            if n + 1 in anchors:
                nxt[0:SUBLANES, 0:LANES] = nxt[0:SUBLANES, 0:LANES] + anchors.pop(n + 1)
        softmax_values(*step, st_bufs[n % 2])

    y = (jnp.dot(conv_out[...], w1_ref[...], preferred_element_type=F32)
         + lax.dot_general(att_t[...], w2_ref[...], CONTRACT_FIRST, preferred_element_type=F32))
    ms = jnp.mean(y * y, axis=-1, keepdims=True)
    o_ref[...] = x_ref[...] + y * lax.rsqrt(ms + RMS_EPS) * pg_ref[...]


def _mixer(x, parts, meta, consts):
    b, t, _ = x.shape
    rows = t // GRID_W
    assert t % TILE_TOKENS == 0 and rows >= WIN_ROWS and rows % 2 == 0
    u, cg, q, k, vt, agt = parts
    nt = t // TILE_TOKENS
    halo_blocks = TILE_TOKENS // CONV_HALO
    kv_blocks = TILE_TOKENS // KV_HALO

    tile = pl.BlockSpec((None, TILE_TOKENS, D_MODEL), lambda bi, i: (bi, i, 0))
    u_prev = pl.BlockSpec((None, CONV_HALO, D_MODEL),
                          lambda bi, i: (bi, jnp.maximum(i * halo_blocks - 1, 0), 0))
    u_next = pl.BlockSpec((None, CONV_HALO, D_MODEL),
                          lambda bi, i: (bi, jnp.minimum((i + 1) * halo_blocks,
                                                         t // CONV_HALO - 1), 0))
    k_prev = pl.BlockSpec((None, KV_HALO, D_MODEL),
                          lambda bi, i: (bi, jnp.maximum(i * kv_blocks - 1, 0), 0))
    k_next = pl.BlockSpec((None, KV_HALO, D_MODEL),
                          lambda bi, i: (bi, jnp.minimum((i + 1) * kv_blocks,
                                                         t // KV_HALO - 1), 0))
    tile_t = pl.BlockSpec((D_MODEL, TILE_TOKENS), lambda bi, i: (0, bi * nt + i))
    vt_prev = pl.BlockSpec((D_MODEL, KV_HALO),
                           lambda bi, i: (0, bi * nt * kv_blocks
                                          + jnp.maximum(i * kv_blocks - 1, 0)))
    vt_next = pl.BlockSpec((D_MODEL, KV_HALO),
                           lambda bi, i: (0, bi * nt * kv_blocks
                                          + jnp.minimum((i + 1) * kv_blocks, t // KV_HALO - 1)))

    def const(a):
        zeros = (0,) * a.ndim
        return pl.BlockSpec(a.shape, lambda bi, i: zeros, pipeline_mode=pl.Buffered(1))

    in_specs = [
        tile,
        tile, u_prev, u_next,
        tile, tile,
        tile, k_prev, k_next,
        tile_t, vt_prev, vt_next,
        tile_t,
    ] + [const(a) for a in meta + consts]
    return pl.pallas_call(
        functools.partial(_mixer_kernel, rows),
        grid=(b, nt),
        in_specs=in_specs,
        out_specs=tile,
        out_shape=jax.ShapeDtypeStruct(x.shape, x.dtype),
        scratch_shapes=[
            pltpu.VMEM((N_SLABS, TILE_TOKENS + 2 * CONV_HALO, LANES), F32),
            pltpu.VMEM((N_SLABS, CONV_GROUP, LANES), F32),
            pltpu.VMEM((KV_WIN, D_MODEL), BF16),
            pltpu.VMEM((KV_WIN // LANES, D_MODEL, LANES), BF16),
            pltpu.VMEM((TILE_TOKENS, D_MODEL), BF16),
            pltpu.VMEM((D_MODEL, TILE_TOKENS), BF16),
            pltpu.VMEM((SCORE_TOKENS + N_META, QUAD_SCORE_LANES), F32),
            pltpu.VMEM((SCORE_TOKENS + N_META, QUAD_SCORE_LANES), F32),
        ],
        compiler_params=pltpu.CompilerParams(
            dimension_semantics=("arbitrary", "arbitrary"),
            vmem_limit_bytes=VMEM_LIMIT),
        name="mixer",
    )(x, u, u, u, cg, q, k, k, k, vt, vt, vt, agt, *meta, *consts)


def _bias_table(rel_bias):
    kc = np.arange(GRID_W)[:, None]
    qc = np.arange(GRID_W)[None, :]
    start = np.clip(qc - WIN_COLS // 2, 0, GRID_W - WIN_COLS)
    valid = (kc >= start) & (kc < start + WIN_COLS)
    padded = jnp.pad(rel_bias.astype(F32), ((0, 0), (0, 0), (GRID_W - WIN_COLS, GRID_W - WIN_COLS)))
    toeplitz = jnp.stack([padded[:, :, GRID_W - 1 - q:2 * GRID_W - 1 - q] for q in range(GRID_W)],
                         axis=-1)
    c = jnp.where(valid[None, None], toeplitz, NEG_INF)
    c = jnp.concatenate([c, jnp.full_like(c[:, :1], NEG_INF)], axis=1)
    return jnp.concatenate([c[:, 1:], c[:, :-1]], axis=-1) * LOG2_E


def _row_penalties():
    pen = np.zeros((4, SUBLANES, LANES), np.float32)
    pen[1, :, GRID_W:] = NEG_INF
    pen[2, :, :GRID_W] = NEG_INF
    pen[3] = NEG_INF
    return jnp.asarray(pen)


def _head_mask():
    r = np.arange(QUAD_SCORE_LANES)[:, None] // PAIR_TOKENS
    c = np.arange(QUAD_LANES)[None, :] // HEAD_DIM
    return jnp.asarray(r == c, BF16)


def _sublane_rows(a):
    return jnp.broadcast_to(a.astype(F32)[..., None, :], a.shape[:-1] + (SUBLANES, a.shape[-1]))


def kernel(x_prompt, x_sample, meta_tokens, pre_norm_g, w_in, conv_w, conv_b,
           conv_ln_g, conv_ln_b, rel_bias, post_norm_g, w_out):
    assert pre_norm_g.shape[0] == 1, "single-layer trunk"
    pre_g = pre_norm_g[0][None]
    w_in_bf = w_in[0].astype(BF16)
    w_rows = w_in_bf[:, :N_ROW_MAJOR * D_MODEL]
    w_cols_t = w_in_bf[:, N_ROW_MAJOR * D_MODEL:].T
    w_out_bf = w_out[0].astype(BF16)
    row = lambda a: a[0][None].astype(F32)

    meta_parts = _project(meta_tokens.astype(F32), pre_g, w_rows, w_cols_t, N_META)
    vm_t = jnp.pad(meta_parts[4], ((0, 0), (0, LANES - N_META)))
    meta = [meta_parts[0], meta_parts[3], vm_t]
    consts = [_sublane_rows(conv_w[0]), _sublane_rows(conv_b[0]), row(conv_ln_g), row(conv_ln_b),
              _bias_table(rel_bias[0]), _row_penalties(), _head_mask(), row(post_norm_g),
              w_out_bf[:D_MODEL], w_out_bf[D_MODEL:]]

    outs = []
    for x in (x_prompt, x_sample):
        b, t, d = x.shape
        parts = _project(x.reshape(b * t, d), pre_g, w_rows, w_cols_t, PROJ_TOKENS)
        parts = [p.reshape(b, t, d) for p in parts[:4]] + list(parts[4:])
        outs.append(_mixer(x, parts, meta, consts))
    return tuple(outs)
```
